```python
import jax, jax.numpy as jnp
from jax import lax
import numpy as np

D_MODEL = 1024
BATCH = 16
SEQ = 4096
DEPTH = 2
DEC_BATCH = 16
DEC_SEQ = 64
PAST_LEN = 1024

CHUNK = 64
N_A_LAYERS = DEPTH // 2
N_B_LAYERS = DEPTH - N_A_LAYERS
POOL_WINDOWS = (2, 4, 8, 16)
N_POOL_GROUPS = len(POOL_WINDOWS)
POOL_GROUP_WIDTH = D_MODEL // N_POOL_GROUPS
POOL_HIST = max(POOL_WINDOWS) - 1
HEAD_DIM = 64
N_HEADS = D_MODEL // HEAD_DIM
ATTN_WIDTH = N_HEADS * HEAD_DIM
D_FF = -(-8 * D_MODEL // (3 * 256)) * 256
Q_BLOCK = 128
RMS_EPS = 1e-6
NEG_INF = -1e30
FORGET_BIAS_INIT = 2.0

kernel_name = "yoco_pool_fox_streaming_step"


def rms_norm(x, g):
    xf = x.astype(jnp.float32)
    y = xf * lax.rsqrt(jnp.mean(xf * xf, axis=-1, keepdims=True) + RMS_EPS)
    return (y * g.astype(jnp.float32)).astype(x.dtype)


def swiglu_ffn(h, w_gate, w_up, w_down):
    return (jax.nn.silu(h @ w_gate) * (h @ w_up)) @ w_down


def pool_mixer(h, prev, pos0, w_groups, scale):
    S = h.shape[1]
    ext = jnp.concatenate([prev.astype(h.dtype), h], axis=1)
    cs = jnp.cumsum(ext.astype(jnp.float32), axis=1)
    cs = jnp.pad(cs, ((0, 0), (1, 0), (0, 0)))
    end = cs[:, POOL_HIST + 1:POOL_HIST + 1 + S]
    pos = pos0 + jnp.arange(S, dtype=jnp.int32)
    hf = h.astype(jnp.float32)
    outs = []
    for g, w in enumerate(POOL_WINDOWS):
        sl = slice(g * POOL_GROUP_WIDTH, (g + 1) * POOL_GROUP_WIDTH)
        start = cs[:, POOL_HIST + 1 - w:POOL_HIST + 1 - w + S, sl]
        cnt = jnp.minimum(pos + 1, w).astype(jnp.float32)[None, :, None]
        pooled = (end[..., sl] - start) / cnt - hf[..., sl]
        outs.append(jnp.einsum('bsc,cd->bsd', pooled, w_groups[g].astype(jnp.float32)))
    out = jnp.concatenate(outs, axis=-1) * scale.astype(jnp.float32)
    return out.astype(h.dtype), ext[:, -POOL_HIST:]


def shared_kv(x, g_kv, w_k, w_v, w_f, b_f, g_knorm):
    B, S, _ = x.shape
    h = rms_norm(x, g_kv)
    k = rms_norm((h @ w_k).reshape(B, S, N_HEADS, HEAD_DIM), g_knorm)
    v = (h @ w_v).reshape(B, S, N_HEADS, HEAD_DIM)
    logf = jax.nn.log_sigmoid((h @ w_f + b_f).astype(jnp.float32))
    return k, v, logf


def fox_attend(q, dq, k, dk, v, q_pos, k_pos):
    s = jnp.einsum('bqhd,bkhd->bhqk', q.astype(jnp.float32), k.astype(jnp.float32)) * (HEAD_DIM ** -0.5)
    s = s + (jnp.transpose(dq, (0, 2, 1))[..., :, None] - jnp.transpose(dk, (0, 2, 1))[..., None, :])
    mask = k_pos[None, :] <= q_pos[:, None]
    p = jax.nn.softmax(jnp.where(mask, s, NEG_INF), axis=-1)
    return jnp.einsum('bhqk,bkhd->bqhd', p, v.astype(jnp.float32))


def fox_prompt(q, k, v, logf):
    B, S, H, Dh = q.shape
    d = jnp.cumsum(logf.astype(jnp.float32), axis=1)
    nblk = S // Q_BLOCK
    qb = q.reshape(B, nblk, Q_BLOCK, H, Dh).transpose(1, 0, 2, 3, 4)
    dqb = d.reshape(B, nblk, Q_BLOCK, H).transpose(1, 0, 2, 3)
    starts = jnp.arange(nblk, dtype=jnp.int32) * Q_BLOCK
    k_pos = jnp.arange(S, dtype=jnp.int32)

    def one_block(args):
        qi, dqi, st = args
        return fox_attend(qi, dqi, k, d, v, st + jnp.arange(Q_BLOCK, dtype=jnp.int32), k_pos)

    o = lax.map(one_block, (qb, dqb, starts))
    return o.transpose(1, 0, 2, 3, 4).reshape(B, S, H * Dh)


def fox_sample(q, k_all, v_all, logf_all, past):
    B, T, H, Dh = q.shape
    d = jnp.cumsum(logf_all.astype(jnp.float32), axis=1)
    k_pos = jnp.arange(past + T, dtype=jnp.int32)
    q_pos = past + jnp.arange(T, dtype=jnp.int32)
    o = fox_attend(q, d[:, past:], k_all, d, v_all, q_pos, k_pos)
    return o.reshape(B, T, H * Dh)


def setup_inputs(seed: int = 0) -> dict:
    key = jax.random.key(seed)
    ks = jax.random.split(key, 24)

    def nrm(k, shape, s):
        return jax.random.normal(k, shape, jnp.float32) * s

    return {
        'x_prompt': nrm(ks[0], (BATCH, SEQ, D_MODEL), 1.0),
        'x_sample': nrm(ks[1], (DEC_BATCH, DEC_SEQ, D_MODEL), 1.0),
        'state_pool': nrm(ks[2], (N_A_LAYERS, DEC_BATCH, POOL_HIST, D_MODEL), 1.0),
        'cache_k': nrm(ks[3], (DEC_BATCH, PAST_LEN, N_HEADS, HEAD_DIM), 1.0),
        'cache_v': nrm(ks[4], (DEC_BATCH, PAST_LEN, N_HEADS, HEAD_DIM), 1.0),
        'cache_logf': jax.nn.log_sigmoid(FORGET_BIAS_INIT + nrm(ks[5], (DEC_BATCH, PAST_LEN, N_HEADS), 1.0)),
        'g_mix': 1.0 + nrm(ks[6], (DEPTH, D_MODEL), 0.02),
        'g_ffn': 1.0 + nrm(ks[7], (DEPTH, D_MODEL), 0.02),
        'pool_w': nrm(ks[8], (N_A_LAYERS, N_POOL_GROUPS, POOL_GROUP_WIDTH, POOL_GROUP_WIDTH), POOL_GROUP_WIDTH ** -0.5),
        'pool_scale': 1.0 + nrm(ks[9], (N_A_LAYERS, D_MODEL), 0.02),
        'g_kv': 1.0 + nrm(ks[10], (D_MODEL,), 0.02),
        'w_k': nrm(ks[11], (D_MODEL, ATTN_WIDTH), D_MODEL ** -0.5),
        'w_v': nrm(ks[12], (D_MODEL, ATTN_WIDTH), D_MODEL ** -0.5),
        'w_f': nrm(ks[13], (D_MODEL, N_HEADS), D_MODEL ** -0.5),
        'b_f': FORGET_BIAS_INIT + nrm(ks[14], (N_HEADS,), 0.1),
        'g_knorm': 1.0 + nrm(ks[15], (HEAD_DIM,), 0.02),
        'w_q': nrm(ks[16], (N_B_LAYERS, D_MODEL, ATTN_WIDTH), D_MODEL ** -0.5),
        'g_qnorm': 1.0 + nrm(ks[17], (N_B_LAYERS, HEAD_DIM), 0.02),
        'w_o': nrm(ks[18], (N_B_LAYERS, ATTN_WIDTH, D_MODEL), ATTN_WIDTH ** -0.5),
        'w_gate': nrm(ks[19], (DEPTH, D_MODEL, D_FF), D_MODEL ** -0.5),
        'w_up': nrm(ks[20], (DEPTH, D_MODEL, D_FF), D_MODEL ** -0.5),
        'w_down': nrm(ks[21], (DEPTH, D_FF, D_MODEL), D_FF ** -0.5),
    }


def reference(x_prompt, x_sample, state_pool, cache_k, cache_v, cache_logf,
              g_mix, g_ffn, pool_w, pool_scale,
              g_kv, w_k, w_v, w_f, b_f, g_knorm,
              w_q, g_qnorm, w_o,
              w_gate, w_up, w_down):
    past = cache_k.shape[1]
    B, S, _ = x_prompt.shape
    Bs, Ss, _ = x_sample.shape
    xp, xs = x_prompt, x_sample
    pool_new_p, pool_new_s = [], []
    for l in range(DEPTH):
        if l < N_A_LAYERS:
            hp = rms_norm(xp, g_mix[l])
            hs = rms_norm(xs, g_mix[l])
            yp, stp = pool_mixer(hp, jnp.zeros((B, POOL_HIST, D_MODEL), hp.dtype), 0, pool_w[l], pool_scale[l])
            ys, sts = pool_mixer(hs, state_pool[l], past, pool_w[l], pool_scale[l])
            pool_new_p.append(stp)
            pool_new_s.append(sts)
            xp = xp + yp
            xs = xs + ys
        else:
            j = l - N_A_LAYERS
            if j == 0:
                k_p, v_p, logf_p = shared_kv(xp, g_kv, w_k, w_v, w_f, b_f, g_knorm)
                k_s, v_s, logf_s = shared_kv(xs, g_kv, w_k, w_v, w_f, b_f, g_knorm)
                k_all = jnp.concatenate([cache_k.astype(k_s.dtype), k_s], axis=1)
                v_all = jnp.concatenate([cache_v.astype(v_s.dtype), v_s], axis=1)
                logf_all = jnp.concatenate([cache_logf.astype(jnp.float32), logf_s], axis=1)
            hp = rms_norm(xp, g_mix[l])
            hs = rms_norm(xs, g_mix[l])
            qp = rms_norm((hp @ w_q[j]).reshape(B, S, N_HEADS, HEAD_DIM), g_qnorm[j])
            qs = rms_norm((hs @ w_q[j]).reshape(Bs, Ss, N_HEADS, HEAD_DIM), g_qnorm[j])
            op = fox_prompt(qp, k_p, v_p, logf_p).astype(xp.dtype)
            os_ = fox_sample(qs, k_all, v_all, logf_all, past).astype(xs.dtype)
            xp = xp + op @ w_o[j]
            xs = xs + os_ @ w_o[j]
        xp = xp + swiglu_ffn(rms_norm(xp, g_ffn[l]), w_gate[l], w_up[l], w_down[l])
        xs = xs + swiglu_ffn(rms_norm(xs, g_ffn[l]), w_gate[l], w_up[l], w_down[l])
    pool_state_p = jnp.stack(pool_new_p, axis=0)
    pool_state_s = jnp.stack(pool_new_s, axis=0)
    return (xp, xs, pool_state_p, pool_state_s, k_p, v_p, logf_p, k_s, v_s, logf_s)
```

```python
import functools

import jax
import jax.numpy as jnp
import numpy as np
from jax import lax
from jax.experimental import pallas as pl
from jax.experimental.pallas import tpu as pltpu

POOL_WINDOWS = (2, 4, 8, 16)
POOL_HIST = max(POOL_WINDOWS) - 1
HIST_ROWS = 16
HEAD_DIM = 64
LANES = 128
RMS_EPS = 1e-6
NEG_INF = -1e30
VMEM_LIMIT_BYTES = 56 * 1024 * 1024

F32 = jnp.float32
BF16 = jnp.bfloat16


def _const_spec(shape):
    return pl.BlockSpec(shape, lambda *_: (0,) * len(shape), pipeline_mode=pl.Buffered(1))


def _rms_scale(x):
    return lax.rsqrt(jnp.mean(x * x, axis=-1, keepdims=True) + RMS_EPS)


def _swiglu(x1, g_ffn, wg_ref, wu_ref, wd_ref):
    h = (x1 * _rms_scale(x1) * g_ffn).astype(BF16)
    gate = jnp.dot(h, wg_ref[...], preferred_element_type=F32)
    up = jnp.dot(h, wu_ref[...], preferred_element_type=F32)
    act = (gate * jax.nn.sigmoid(gate) * up).astype(BF16)
    return x1 + jnp.dot(act, wd_ref[...], preferred_element_type=F32)


def _layer0_kernel(x_ref, prev_ref, gmix_ref, poolw_ref, pscale_ref, gffn_ref, wg_ref, wu_ref, wd_ref,
                   y_ref, state_ref, ext_ref, *, pos0, tile):
    j = pl.program_id(1)
    x = x_ref[0]
    h = x * _rms_scale(x) * gmix_ref[...]

    @pl.when(j == 0)
    def _():
        ext_ref[0:HIST_ROWS, :] = prev_ref[0]

    @pl.when(j > 0)
    def _():
        ext_ref[0:HIST_ROWS, :] = ext_ref[tile:tile + HIST_ROWS, :]

    ext_ref[HIST_ROWS:HIST_ROWS + tile, :] = h
    state_ref[0] = ext_ref[tile:tile + HIST_ROWS, :]

    pos = pos0 + j * tile + lax.broadcasted_iota(jnp.int32, (tile, 1), 0)
    width = x.shape[-1] // len(POOL_WINDOWS)
    outs = []
    for g, w in enumerate(POOL_WINDOWS):
        c0 = g * width
        hg = ext_ref[HIST_ROWS:HIST_ROWS + tile, c0:c0 + width]
        win = hg
        for k in range(1, w):
            win = win + ext_ref[HIST_ROWS - k:HIST_ROWS - k + tile, c0:c0 + width]
        cnt = jnp.minimum(pos + 1, w).astype(F32)
        pooled = win / cnt - hg
        outs.append(jnp.dot(pooled.astype(BF16), poolw_ref[g], preferred_element_type=F32))
    x1 = x + jnp.concatenate(outs, axis=-1) * pscale_ref[...]
    y_ref[0] = _swiglu(x1, gffn_ref[...], wg_ref, wu_ref, wd_ref)


def _layer0(x, prev, g_mix, pool_w, pool_scale, g_ffn, wg, wu, wd, *, pos0, tile):
    B, S, D = x.shape
    F = wg.shape[1]
    G, W = pool_w.shape[0], pool_w.shape[1]
    kern = functools.partial(_layer0_kernel, pos0=pos0, tile=tile)
    return pl.pallas_call(
        kern,
        grid=(B, S // tile),
        in_specs=[
            pl.BlockSpec((1, tile, D), lambda b, j: (b, j, 0)),
            pl.BlockSpec((1, HIST_ROWS, D), lambda b, j: (b, 0, 0)),
            _const_spec((1, D)),
            _const_spec((G, W, W)),
            _const_spec((1, D)),
            _const_spec((1, D)),
            _const_spec((D, F)),
            _const_spec((D, F)),
            _const_spec((F, D)),
        ],
        out_specs=[
            pl.BlockSpec((1, tile, D), lambda b, j: (b, j, 0)),
            pl.BlockSpec((1, HIST_ROWS, D), lambda b, j: (b, 0, 0)),
        ],
        out_shape=[
            jax.ShapeDtypeStruct((B, S, D), F32),
            jax.ShapeDtypeStruct((B, HIST_ROWS, D), F32),
        ],
        scratch_shapes=[pltpu.VMEM((HIST_ROWS + tile, D), F32)],
        compiler_params=pltpu.CompilerParams(
            dimension_semantics=("arbitrary", "arbitrary"), vmem_limit_bytes=VMEM_LIMIT_BYTES),
        name="layer0_pool_ffn",
    )(x, prev, g_mix, pool_w, pool_scale, g_ffn, wg, wu, wd)


def _proj_kernel(x_ref, gkv_ref, gq_ref, wk_ref, wv_ref, wf_ref, bf_ref, wq_ref, gk_ref, gqn_ref, avg_ref,
                 k_ref, v_ref, logf_ref, qb_ref, kb_ref, vb_ref, *, n_heads):
    x = x_ref[0]
    xn = x * _rms_scale(x)
    hkv = (xn * gkv_ref[...]).astype(BF16)
    hq = (xn * gq_ref[...]).astype(BF16)

    def head_norm(y, g):
        ms = jnp.dot((y * y).astype(BF16), avg_ref[...], preferred_element_type=F32)
        return y * lax.rsqrt(ms + RMS_EPS) * g

    k = head_norm(jnp.dot(hkv, wk_ref[...], preferred_element_type=F32), gk_ref[...])
    k_ref[0] = k
    kb_ref[0] = k.astype(BF16)
    v = jnp.dot(hkv, wv_ref[...], preferred_element_type=F32)
    v_ref[0] = v
    vb_ref[0] = v.astype(BF16)
    z = jnp.dot(hkv, wf_ref[...], preferred_element_type=F32)[:, :n_heads] + bf_ref[...]
    logf_ref[0] = -(jnp.maximum(-z, 0.0) + jnp.log1p(jnp.exp(-jnp.abs(z))))
    q = head_norm(jnp.dot(hq, wq_ref[...], preferred_element_type=F32), gqn_ref[...])
    qb_ref[0] = (q * (HEAD_DIM ** -0.5)).astype(BF16)


def _proj(x, g_kv, g_q, wk, wv, wf, bf, wq, gk, gqn, avg, *, tile):
    B, S, D = x.shape
    A = wk.shape[1]
    H = bf.shape[1]
    tok = lambda b, j: (b, j, 0)
    return pl.pallas_call(
        functools.partial(_proj_kernel, n_heads=H),
        grid=(B, S // tile),
        in_specs=[
            pl.BlockSpec((1, tile, D), tok),
            _const_spec((1, D)), _const_spec((1, D)),
            _const_spec((D, A)), _const_spec((D, A)), _const_spec((D, LANES)), _const_spec((1, H)),
            _const_spec((D, A)), _const_spec((1, A)), _const_spec((1, A)), _const_spec((A, A)),
        ],
        out_specs=[
            pl.BlockSpec((1, tile, A), tok), pl.BlockSpec((1, tile, A), tok), pl.BlockSpec((1, tile, H), tok),
            pl.BlockSpec((1, tile, A), tok), pl.BlockSpec((1, tile, A), tok), pl.BlockSpec((1, tile, A), tok),
        ],
        out_shape=[
            jax.ShapeDtypeStruct((B, S, A), F32), jax.ShapeDtypeStruct((B, S, A), F32),
            jax.ShapeDtypeStruct((B, S, H), F32),
            jax.ShapeDtypeStruct((B, S, A), BF16), jax.ShapeDtypeStruct((B, S, A), BF16),
            jax.ShapeDtypeStruct((B, S, A), BF16),
        ],
        compiler_params=pltpu.CompilerParams(
            dimension_semantics=("arbitrary", "arbitrary"), vmem_limit_bytes=VMEM_LIMIT_BYTES),
        name="kvq_proj",
    )(x, g_kv, g_q, wk, wv, wf, bf, wq, gk, gqn, avg)


def _cumsum_kernel(x_ref, o_ref, carry_ref, *, tile):
    j = pl.program_id(1)

    @pl.when(j == 0)
    def _():
        carry_ref[...] = jnp.zeros_like(carry_ref)

    x = x_ref[0]
    hi = x.astype(BF16)
    r1 = x - hi.astype(F32)
    mid = r1.astype(BF16)
    lo = (r1 - mid.astype(F32)).astype(BF16)
    row = lax.broadcasted_iota(jnp.int32, (tile, tile), 0)
    col = lax.broadcasted_iota(jnp.int32, (tile, tile), 1)
    tri = (col <= row).astype(BF16)
    cs = (jnp.dot(tri, hi, preferred_element_type=F32) + jnp.dot(tri, mid, preferred_element_type=F32)
          + jnp.dot(tri, lo, preferred_element_type=F32)) + carry_ref[...]
    o_ref[0] = cs
    carry_ref[...] = cs[tile - 1:tile, :]


def _cumsum(x, *, tile):
    B, S, H = x.shape
    return pl.pallas_call(
        functools.partial(_cumsum_kernel, tile=tile),
        grid=(B, S // tile),
        in_specs=[pl.BlockSpec((1, tile, H), lambda b, j: (b, j, 0))],
        out_specs=pl.BlockSpec((1, tile, H), lambda b, j: (b, j, 0)),
        out_shape=jax.ShapeDtypeStruct((B, S, H), F32),
        scratch_shapes=[pltpu.VMEM((1, H), F32)],
        compiler_params=pltpu.CompilerParams(dimension_semantics=("arbitrary", "arbitrary")),
        name="logf_cumsum",
    )(x)


def _flash_kernel(qi_tab, ki_tab, q_ref, k_ref, v_ref, dq_ref, dk_ref, o_ref, m_ref, l_ref, acc_ref,
                  *, tq, tk, q_off):
    t = pl.program_id(2)
    qi = qi_tab[t]
    ki = ki_tab[t]
    last_ki = (q_off + qi * tq + tq - 1) // tk

    @pl.when(ki == 0)
    def _():
        m_ref[...] = jnp.full_like(m_ref, NEG_INF)
        l_ref[...] = jnp.zeros_like(l_ref)
        acc_ref[...] = jnp.zeros_like(acc_ref)

    q = q_ref[0]
    k = k_ref[0]
    v = v_ref[0]
    lane = lax.broadcasted_iota(jnp.int32, (1, LANES), 1)
    q_pos = q_off + qi * tq + lax.broadcasted_iota(jnp.int32, (tq, tk), 0)
    k_pos = ki * tk + lax.broadcasted_iota(jnp.int32, (tq, tk), 1)
    visible = k_pos <= q_pos
    alphas = []
    pvs = []
    for h in range(2):
        head_lanes = (lane < HEAD_DIM) if h == 0 else (lane >= HEAD_DIM)
        qh = jnp.where(head_lanes, q, jnp.zeros_like(q))
        s = lax.dot_general(qh, k, (((1,), (1,)), ((), ())), preferred_element_type=F32)
        s = s + (dq_ref[0, 0][:, h:h + 1] - dk_ref[0, 0][h:h + 1, :])
        s = jnp.where(visible, s, NEG_INF)
        m_prev = m_ref[h]
        m_new = jnp.maximum(m_prev, jnp.max(s, axis=1, keepdims=True))
        p = jnp.exp(s - m_new)
        alpha = jnp.exp(m_prev - m_new)
        l_ref[h] = alpha * l_ref[h] + jnp.sum(p, axis=1, keepdims=True)
        m_ref[h] = m_new
        alphas.append(alpha)
        pvs.append(jnp.dot(p.astype(BF16), v, preferred_element_type=F32))
    first = lane < HEAD_DIM
    acc_ref[...] = jnp.where(first, alphas[0], alphas[1]) * acc_ref[...] + jnp.where(first, pvs[0], pvs[1])

    @pl.when(ki == last_ki)
    def _():
        o_ref[0] = (acc_ref[...] / jnp.where(first, l_ref[0], l_ref[1])).astype(o_ref.dtype)


def _flash(q, k, v, dq, dk, *, tq, tk, q_off):
    B, Sq, A = q.shape
    HP = A // LANES
    nq = Sq // tq
    pairs = [(i, j) for i in range(nq) for j in range((q_off + i * tq + tq - 1) // tk + 1)]
    qi_tab = jnp.asarray(np.array([p[0] for p in pairs], np.int32))
    ki_tab = jnp.asarray(np.array([p[1] for p in pairs], np.int32))
    grid_spec = pltpu.PrefetchScalarGridSpec(
        num_scalar_prefetch=2,
        grid=(B, HP, len(pairs)),
        in_specs=[
            pl.BlockSpec((1, tq, LANES), lambda b, hp, t, qt, kt: (b, qt[t], hp)),
            pl.BlockSpec((1, tk, LANES), lambda b, hp, t, qt, kt: (b, kt[t], hp)),
            pl.BlockSpec((1, tk, LANES), lambda b, hp, t, qt, kt: (b, kt[t], hp)),
            pl.BlockSpec((1, 1, tq, 2), lambda b, hp, t, qt, kt: (b, hp, qt[t], 0)),
            pl.BlockSpec((1, 1, 2, tk), lambda b, hp, t, qt, kt: (b, hp, 0, kt[t])),
        ],
        out_specs=pl.BlockSpec((1, tq, LANES), lambda b, hp, t, qt, kt: (b, qt[t], hp)),
        scratch_shapes=[
            pltpu.VMEM((2, tq, 1), F32), pltpu.VMEM((2, tq, 1), F32), pltpu.VMEM((tq, LANES), F32)],
    )
    return pl.pallas_call(
        functools.partial(_flash_kernel, tq=tq, tk=tk, q_off=q_off),
        grid_spec=grid_spec,
        out_shape=jax.ShapeDtypeStruct((B, Sq, A), BF16),
        compiler_params=pltpu.CompilerParams(
            dimension_semantics=("arbitrary", "arbitrary", "arbitrary"), vmem_limit_bytes=VMEM_LIMIT_BYTES),
        name="fox_flash",
    )(qi_tab, ki_tab, q, k, v, dq, dk)


def _layer1_kernel(x_ref, o_ref, wo_ref, gffn_ref, wg_ref, wu_ref, wd_ref, y_ref):
    x1 = x_ref[0] + jnp.dot(o_ref[0], wo_ref[...], preferred_element_type=F32)
    y_ref[0] = _swiglu(x1, gffn_ref[...], wg_ref, wu_ref, wd_ref)


def _layer1(x, o, wo, g_ffn, wg, wu, wd, *, tile):
    B, S, D = x.shape
    A = o.shape[-1]
    F = wg.shape[1]
    tok = lambda b, j: (b, j, 0)
    return pl.pallas_call(
        _layer1_kernel,
        grid=(B, S // tile),
        in_specs=[
            pl.BlockSpec((1, tile, D), tok), pl.BlockSpec((1, tile, A), tok),
            _const_spec((A, D)), _const_spec((1, D)),
            _const_spec((D, F)), _const_spec((D, F)), _const_spec((F, D)),
        ],
        out_specs=pl.BlockSpec((1, tile, D), tok),
        out_shape=jax.ShapeDtypeStruct((B, S, D), F32),
        compiler_params=pltpu.CompilerParams(
            dimension_semantics=("arbitrary", "arbitrary"), vmem_limit_bytes=VMEM_LIMIT_BYTES),
        name="layer1_out_ffn",
    )(x, o, wo, g_ffn, wg, wu, wd)


def _pair_layouts(d):
    B, S, H = d.shape
    d4 = d.reshape(B, S, H // 2, 2)
    return jnp.transpose(d4, (0, 2, 1, 3)), jnp.transpose(d4, (0, 2, 3, 1))


def kernel(x_prompt, x_sample, state_pool, cache_k, cache_v, cache_logf, g_mix, g_ffn, pool_w, pool_scale,
           g_kv, w_k, w_v, w_f, b_f, g_knorm, w_q, g_qnorm, w_o, w_gate, w_up, w_down):
    B, S, D = x_prompt.shape
    Bs, Ss, _ = x_sample.shape
    past = cache_k.shape[1]
    H = cache_k.shape[2]
    A = H * HEAD_DIM
    n_pool = state_pool.shape[0]
    assert n_pool == 1 and g_mix.shape[0] == 2 and w_q.shape[0] == 1, "one pooling layer then one attention layer"

    row = lambda a: a.reshape(1, -1).astype(F32)
    wg = w_gate.astype(BF16)
    wu = w_up.astype(BF16)
    wd = w_down.astype(BF16)
    pw = pool_w[0].astype(BF16)
    wk = w_k.astype(BF16)
    wv = w_v.astype(BF16)
    wq = w_q[0].astype(BF16)
    wo = w_o[0].astype(BF16)
    wf = jnp.pad(w_f, ((0, 0), (0, LANES - H))).astype(BF16)
    head_of = jnp.arange(A, dtype=jnp.int32) // HEAD_DIM
    avg = jnp.where(head_of[:, None] == head_of[None, :], 1.0 / HEAD_DIM, 0.0).astype(BF16)
    gk = row(jnp.tile(g_knorm, H))
    gqn = row(jnp.tile(g_qnorm[0], H))

    tile_p = 512
    tile_s = Ss

    def stream(x, prev, pos0, tile):
        x2, st = _layer0(x, prev, row(g_mix[0]), pw, row(pool_scale[0]), row(g_ffn[0]), wg[0], wu[0], wd[0],
                         pos0=pos0, tile=tile)
        k, v, logf, qb, kb, vb = _proj(x2, row(g_kv), row(g_mix[1]), wk, wv, wf, row(b_f), wq, gk, gqn, avg,
                                       tile=tile)
        return x2, st[:, HIST_ROWS - POOL_HIST:], k, v, logf, qb, kb, vb

    zero_prev = jnp.zeros((B, HIST_ROWS, D), F32)
    xp2, st_p, k_p, v_p, logf_p, qb_p, kb_p, vb_p = stream(x_prompt, zero_prev, 0, tile_p)
    prev_s = jnp.pad(state_pool[0], ((0, 0), (HIST_ROWS - POOL_HIST, 0), (0, 0)))
    xs2, st_s, k_s, v_s, logf_s, qb_s, kb_s, vb_s = stream(x_sample, prev_s, past, tile_s)

    d_p = _cumsum(logf_p, tile=512)
    dq_p, dk_p = _pair_layouts(d_p)
    o_p = _flash(qb_p, kb_p, vb_p, dq_p, dk_p, tq=512, tk=512, q_off=0)
    y_p = _layer1(xp2, o_p, wo, row(g_ffn[1]), wg[1], wu[1], wd[1], tile=tile_p)

    sk = past + Ss
    sk_pad = -(-sk // LANES) * LANES
    pad = ((0, 0), (0, sk_pad - sk), (0, 0))
    k_all = jnp.pad(jnp.concatenate([cache_k.reshape(Bs, past, A).astype(BF16), kb_s], axis=1), pad)
    v_all = jnp.pad(jnp.concatenate([cache_v.reshape(Bs, past, A).astype(BF16), vb_s], axis=1), pad)
    logf_all = jnp.pad(jnp.concatenate([cache_logf.astype(F32), logf_s], axis=1), pad)
    d_all = _cumsum(logf_all, tile=sk_pad)
    dq_all, dk_s = _pair_layouts(d_all)
    dq_s = dq_all[:, :, past:sk, :]
    o_s = _flash(qb_s, k_all, v_all, dq_s, dk_s, tq=Ss, tk=sk_pad, q_off=past)
    y_s = _layer1(xs2, o_s, wo, row(g_ffn[1]), wg[1], wu[1], wd[1], tile=tile_s)

    return (y_p, y_s, st_p[None], st_s[None],
            k_p.reshape(B, S, H, HEAD_DIM), v_p.reshape(B, S, H, HEAD_DIM), logf_p,
            k_s.reshape(Bs, Ss, H, HEAD_DIM), v_s.reshape(Bs, Ss, H, HEAD_DIM), logf_s)
```

```python
import functools
import math

import jax
import jax.numpy as jnp
from jax import lax
from jax.experimental import pallas as pl
from jax.experimental.pallas import tpu as pltpu

POOL_WINDOWS = (2, 4, 8, 16)
POOL_HIST = max(POOL_WINDOWS) - 1
HIST_ROWS = 16
HEAD_DIM = 64
LANES = 128
SUBLANES = 8
MXU_DIM = 256
KEY_SUB = 128
ROW_CHUNK = 16
RMS_EPS = 1e-6
NEG_INF = -1e30
LOG2E = math.log2(math.e)
VMEM_LIMIT_BYTES = 56 * 1024 * 1024

F32 = jnp.float32
BF16 = jnp.bfloat16


def _const_spec(shape):
    return pl.BlockSpec(shape, lambda *_: (0,) * len(shape), pipeline_mode=pl.Buffered(1))


def _rms_scale(x):
    return lax.rsqrt(jnp.mean(x * x, axis=-1, keepdims=True) + RMS_EPS)


def _swiglu(x1, g_ffn, wg_ref, wu_ref, wd_ref):
    h = (x1 * _rms_scale(x1) * g_ffn).astype(BF16)
    gate = jnp.dot(h, wg_ref[...], preferred_element_type=F32)
    up = jnp.dot(h, wu_ref[...], preferred_element_type=F32)
    act = (gate * jax.nn.sigmoid(gate) * up).astype(BF16)
    return x1 + jnp.dot(act, wd_ref[...], preferred_element_type=F32)


def _layer0_kernel(x_ref, prev_ref, gmix_ref, poolw_ref, pscale_ref, gffn_ref, wg_ref, wu_ref, wd_ref,
                   y_ref, state_ref, ext_ref, *, pos0, tile):
    j = pl.program_id(1)
    x = x_ref[0]
    h = x * _rms_scale(x) * gmix_ref[...]

    @pl.when(j == 0)
    def _():
        ext_ref[0:HIST_ROWS, :] = prev_ref[0]

    @pl.when(j > 0)
    def _():
        ext_ref[0:HIST_ROWS, :] = ext_ref[tile:tile + HIST_ROWS, :]

    ext_ref[HIST_ROWS:HIST_ROWS + tile, :] = h
    state_ref[0] = ext_ref[tile:tile + HIST_ROWS, :]

    pos = pos0 + j * tile + lax.broadcasted_iota(jnp.int32, (tile, 1), 0)
    width = x.shape[-1] // len(POOL_WINDOWS)
    outs = []
    for g, w in enumerate(POOL_WINDOWS):
        c0 = g * width
        hg = ext_ref[HIST_ROWS:HIST_ROWS + tile, c0:c0 + width]
        win = hg
        for k in range(1, w):
            win = win + ext_ref[HIST_ROWS - k:HIST_ROWS - k + tile, c0:c0 + width]
        cnt = jnp.minimum(pos + 1, w).astype(F32)
        pooled = win / cnt - hg
        outs.append(jnp.dot(pooled.astype(BF16), poolw_ref[g], preferred_element_type=F32))
    x1 = x + jnp.concatenate(outs, axis=-1) * pscale_ref[...]
    y_ref[0] = _swiglu(x1, gffn_ref[...], wg_ref, wu_ref, wd_ref)


def _layer0(x, prev, g_mix, pool_w, pool_scale, g_ffn, wg, wu, wd, *, pos0, tile):
    B, S, D = x.shape
    F = wg.shape[1]
    G, W = pool_w.shape[0], pool_w.shape[1]
    kern = functools.partial(_layer0_kernel, pos0=pos0, tile=tile)
    return pl.pallas_call(
        kern,
        grid=(B, S // tile),
        in_specs=[
            pl.BlockSpec((1, tile, D), lambda b, j: (b, j, 0)),
            pl.BlockSpec((1, HIST_ROWS, D), lambda b, j: (b, 0, 0)),
            _const_spec((1, D)),
            _const_spec((G, W, W)),
            _const_spec((1, D)),
            _const_spec((1, D)),
            _const_spec((D, F)),
            _const_spec((D, F)),
            _const_spec((F, D)),
        ],
        out_specs=[
            pl.BlockSpec((1, tile, D), lambda b, j: (b, j, 0)),
            pl.BlockSpec((1, HIST_ROWS, D), lambda b, j: (b, 0, 0)),
        ],
        out_shape=[
            jax.ShapeDtypeStruct((B, S, D), F32),
            jax.ShapeDtypeStruct((B, HIST_ROWS, D), F32),
        ],
        scratch_shapes=[pltpu.VMEM((HIST_ROWS + tile, D), F32)],
        compiler_params=pltpu.CompilerParams(
            dimension_semantics=("arbitrary", "arbitrary"), vmem_limit_bytes=VMEM_LIMIT_BYTES),
        name="layer0_pool_ffn",
    )(x, prev, g_mix, pool_w, pool_scale, g_ffn, wg, wu, wd)


def _proj_kernel(x_ref, gkv_ref, gq_ref, wk_ref, wv_ref, wf_ref, bf_ref, wq_ref, gk_ref, gqn_ref, avg_ref,
                 k_ref, v_ref, logf_ref, qb_ref, kb_ref, vt_ref, *, n_heads):
    x = x_ref[0]
    xn = x * _rms_scale(x)
    hkv = (xn * gkv_ref[...]).astype(BF16)
    hq = (xn * gq_ref[...]).astype(BF16)

    def head_norm(y, g):
        ms = jnp.dot((y * y).astype(BF16), avg_ref[...], preferred_element_type=F32)
        return y * lax.rsqrt(ms + RMS_EPS) * g

    k = head_norm(jnp.dot(hkv, wk_ref[...], preferred_element_type=F32), gk_ref[...])
    k_ref[0] = k
    kb_ref[0] = k.astype(BF16)
    v = jnp.dot(hkv, wv_ref[...], preferred_element_type=F32)
    v_ref[0] = v
    vt_ref[0] = v.T.astype(BF16)
    z = jnp.dot(hkv, wf_ref[...], preferred_element_type=F32)[:, :n_heads] + bf_ref[...]
    logf_ref[0] = -(jnp.maximum(-z, 0.0) + jnp.log1p(jnp.exp(-jnp.abs(z))))
    q = head_norm(jnp.dot(hq, wq_ref[...], preferred_element_type=F32), gqn_ref[...])
    qb_ref[0] = (q * (HEAD_DIM ** -0.5 * LOG2E)).astype(BF16)


def _proj(x, g_kv, g_q, wk, wv, wf, bf, wq, gk, gqn, avg, *, tile):
    B, S, D = x.shape
    A = wk.shape[1]
    H = bf.shape[1]
    tok = lambda b, j: (b, j, 0)
    return pl.pallas_call(
        functools.partial(_proj_kernel, n_heads=H),
        grid=(B, S // tile),
        in_specs=[
            pl.BlockSpec((1, tile, D), tok),
            _const_spec((1, D)), _const_spec((1, D)),
            _const_spec((D, A)), _const_spec((D, A)), _const_spec((D, LANES)), _const_spec((1, H)),
            _const_spec((D, A)), _const_spec((1, A)), _const_spec((1, A)), _const_spec((A, A)),
        ],
        out_specs=[
            pl.BlockSpec((1, tile, A), tok), pl.BlockSpec((1, tile, A), tok), pl.BlockSpec((1, tile, H), tok),
            pl.BlockSpec((1, tile, A), tok), pl.BlockSpec((1, tile, A), tok),
            pl.BlockSpec((1, A, tile), lambda b, j: (b, 0, j)),
        ],
        out_shape=[
            jax.ShapeDtypeStruct((B, S, A), F32), jax.ShapeDtypeStruct((B, S, A), F32),
            jax.ShapeDtypeStruct((B, S, H), F32),
            jax.ShapeDtypeStruct((B, S, A), BF16), jax.ShapeDtypeStruct((B, S, A), BF16),
            jax.ShapeDtypeStruct((B, A, S), BF16),
        ],
        compiler_params=pltpu.CompilerParams(
            dimension_semantics=("arbitrary", "arbitrary"), vmem_limit_bytes=VMEM_LIMIT_BYTES),
        name="kvq_proj",
    )(x, g_kv, g_q, wk, wv, wf, bf, wq, gk, gqn, avg)


def _cumsum_kernel(x_ref, o_ref, carry_ref, *, tile):
    j = pl.program_id(1)

    @pl.when(j == 0)
    def _():
        carry_ref[...] = jnp.zeros_like(carry_ref)

    x = x_ref[0]
    hi = x.astype(BF16)
    r1 = x - hi.astype(F32)
    mid = r1.astype(BF16)
    lo = (r1 - mid.astype(F32)).astype(BF16)
    row = lax.broadcasted_iota(jnp.int32, (tile, tile), 0)
    col = lax.broadcasted_iota(jnp.int32, (tile, tile), 1)
    tri = (col <= row).astype(BF16)
    cs = (jnp.dot(tri, hi, preferred_element_type=F32) + jnp.dot(tri, mid, preferred_element_type=F32)
          + jnp.dot(tri, lo, preferred_element_type=F32)) + carry_ref[...]
    o_ref[0] = cs * LOG2E
    carry_ref[...] = cs[tile - 1:tile, :]


def _cumsum_log2(x, *, tile):
    B, S, H = x.shape
    return pl.pallas_call(
        functools.partial(_cumsum_kernel, tile=tile),
        grid=(B, S // tile),
        in_specs=[pl.BlockSpec((1, tile, H), lambda b, j: (b, j, 0))],
        out_specs=pl.BlockSpec((1, tile, H), lambda b, j: (b, j, 0)),
        out_shape=jax.ShapeDtypeStruct((B, S, H), F32),
        scratch_shapes=[pltpu.VMEM((1, H), F32)],
        compiler_params=pltpu.CompilerParams(dimension_semantics=("arbitrary", "arbitrary")),
        name="logf_cumsum",
    )(x)


def _flash_kernel(q_ref, k_ref, vt_ref, dq_ref, dk_ref, o_ref, qh_ref, u_ref, p_ref, m_ref, l_ref, acc_ref,
                  *, tq, tk, q_off):
    qi = pl.program_id(2)
    q_start = q_off + qi * tq
    n_full = q_start // tk
    n_masked = (q_off % tk + tq - 1) // tk + 1
    n_sub = tk // KEY_SUB

    q = q_ref[0]
    lane = lax.broadcasted_iota(jnp.int32, (1, LANES), 1)
    zero = jnp.zeros_like(q)
    qh_ref[0] = jnp.where(lane < HEAD_DIM, q, zero)
    qh_ref[1] = jnp.where(lane >= HEAD_DIM, q, zero)
    dq = dq_ref[0, 0]

    m_ref[...] = jnp.full_like(m_ref, NEG_INF)
    l_ref[...] = jnp.zeros_like(l_ref)
    acc_ref[...] = jnp.zeros_like(acc_ref)

    def score_phase(k_start, h, masked):
        m_acc = jnp.full((ROW_CHUNK, tq), NEG_INF, F32)
        if masked:
            q_pos = q_start + lax.broadcasted_iota(jnp.int32, (ROW_CHUNK, tq), 1)
            k_iota = lax.broadcasted_iota(jnp.int32, (ROW_CHUNK, tq), 0)
        for sb in range(n_sub):
            r0 = sb * KEY_SUB
            kb = k_ref[0, pl.ds(k_start + r0, KEY_SUB), :]
            dk = dk_ref[0, 0, pl.ds(k_start + r0, KEY_SUB), :]
            s = lax.dot_general(kb, qh_ref[h], (((1,), (1,)), ((), ())), preferred_element_type=F32)
            for r in range(0, KEY_SUB, ROW_CHUNK):
                u = s[r:r + ROW_CHUNK, :] - dk[r:r + ROW_CHUNK, h:h + 1]
                if masked:
                    u = jnp.where(k_iota + (k_start + r0 + r) <= q_pos, u, NEG_INF)
                u_ref[h, r0 + r:r0 + r + ROW_CHUNK, :] = u
                m_acc = jnp.maximum(m_acc, u)
        return jnp.max(m_acc, axis=0, keepdims=True)

    def softmax_phase(h, m_blk):
        dq_h = dq[h:h + 1, :]
        m_old = m_ref[h, 0:1, :]
        m_new = jnp.maximum(m_old, m_blk + dq_h)
        alpha = jnp.exp2(m_old - m_new)
        shift = jnp.broadcast_to(dq_h - m_new, (ROW_CHUNK, tq))
        l_acc = jnp.zeros((ROW_CHUNK, tq), F32)
        for r in range(0, tk, ROW_CHUNK):
            p = jnp.exp2(u_ref[h, r:r + ROW_CHUNK, :] + shift)
            p_ref[h, r:r + ROW_CHUNK, :] = p.astype(BF16)
            l_acc = l_acc + p
        l_ref[h, 0:1, :] = alpha * l_ref[h, 0:1, :] + jnp.sum(l_acc, axis=0, keepdims=True)
        m_ref[h, 0:1, :] = m_new
        return alpha

    def value_phase(k_start, h, alpha):
        rows = slice(h * HEAD_DIM, (h + 1) * HEAD_DIM)
        pv = jnp.dot(vt_ref[0, rows, pl.ds(k_start, tk)], p_ref[h], preferred_element_type=F32)
        acc_ref[rows, :] = alpha * acc_ref[rows, :] + pv

    def block(j, masked):
        k_start = pl.multiple_of(j * tk, tk)
        m_blks = [score_phase(k_start, h, masked) for h in range(2)]
        for h in range(2):
            value_phase(k_start, h, softmax_phase(h, m_blks[h]))

    def full_block(j, carry):
        block(j, False)
        return carry

    lax.fori_loop(0, n_full, full_block, 0)
    for d in range(n_masked):
        block(n_full + d, True)

    out_t = jnp.concatenate(
        [acc_ref[h * HEAD_DIM:(h + 1) * HEAD_DIM, :] / l_ref[h, 0:1, :] for h in range(2)], axis=0)
    o_ref[0] = out_t.T.astype(o_ref.dtype)


def _flash(q, k, vt, dq, dk, *, tq, tk, q_off):
    B, Sq, A = q.shape
    Sk = k.shape[1]
    HP = A // LANES
    assert Sq % tq == 0 and Sk % tk == 0 and (tq == tk or Sk == tk)
    return pl.pallas_call(
        functools.partial(_flash_kernel, tq=tq, tk=tk, q_off=q_off),
        grid=(B, HP, Sq // tq),
        in_specs=[
            pl.BlockSpec((1, tq, LANES), lambda b, hp, i: (b, i, hp)),
            pl.BlockSpec((1, Sk, LANES), lambda b, hp, i: (b, 0, hp)),
            pl.BlockSpec((1, LANES, Sk), lambda b, hp, i: (b, hp, 0)),
            pl.BlockSpec((1, 1, 2, tq), lambda b, hp, i: (b, hp, 0, i)),
            pl.BlockSpec((1, 1, Sk, 2), lambda b, hp, i: (b, hp, 0, 0)),
        ],
        out_specs=pl.BlockSpec((1, tq, LANES), lambda b, hp, i: (b, i, hp)),
        out_shape=jax.ShapeDtypeStruct((B, Sq, A), BF16),
        scratch_shapes=[
            pltpu.VMEM((2, tq, LANES), BF16), pltpu.VMEM((2, tk, tq), F32), pltpu.VMEM((2, tk, tq), BF16),
            pltpu.VMEM((2, SUBLANES, tq), F32), pltpu.VMEM((2, SUBLANES, tq), F32),
            pltpu.VMEM((LANES, tq), F32)],
        compiler_params=pltpu.CompilerParams(
            dimension_semantics=("arbitrary", "arbitrary", "arbitrary"), vmem_limit_bytes=VMEM_LIMIT_BYTES),
        name="fox_flash",
    )(q, k, vt, dq, dk)


def _layer1_kernel(x_ref, o_ref, wo_ref, gffn_ref, wg_ref, wu_ref, wd_ref, y_ref):
    x1 = x_ref[0] + jnp.dot(o_ref[0], wo_ref[...], preferred_element_type=F32)
    y_ref[0] = _swiglu(x1, gffn_ref[...], wg_ref, wu_ref, wd_ref)


def _layer1(x, o, wo, g_ffn, wg, wu, wd, *, tile):
    B, S, D = x.shape
    A = o.shape[-1]
    F = wg.shape[1]
    tok = lambda b, j: (b, j, 0)
    return pl.pallas_call(
        _layer1_kernel,
        grid=(B, S // tile),
        in_specs=[
            pl.BlockSpec((1, tile, D), tok), pl.BlockSpec((1, tile, A), tok),
            _const_spec((A, D)), _const_spec((1, D)),
            _const_spec((D, F)), _const_spec((D, F)), _const_spec((F, D)),
        ],
        out_specs=pl.BlockSpec((1, tile, D), tok),
        out_shape=jax.ShapeDtypeStruct((B, S, D), F32),
        compiler_params=pltpu.CompilerParams(
            dimension_semantics=("arbitrary", "arbitrary"), vmem_limit_bytes=VMEM_LIMIT_BYTES),
        name="layer1_out_ffn",
    )(x, o, wo, g_ffn, wg, wu, wd)


def _pair_layouts(d):
    B, S, H = d.shape
    d4 = d.reshape(B, S, H // 2, 2)
    return jnp.transpose(d4, (0, 2, 3, 1)), jnp.transpose(d4, (0, 2, 1, 3))


def kernel(x_prompt, x_sample, state_pool, cache_k, cache_v, cache_logf, g_mix, g_ffn, pool_w, pool_scale,
           g_kv, w_k, w_v, w_f, b_f, g_knorm, w_q, g_qnorm, w_o, w_gate, w_up, w_down):
    B, S, D = x_prompt.shape
    Bs, Ss, _ = x_sample.shape
    past = cache_k.shape[1]
    H = cache_k.shape[2]
    A = H * HEAD_DIM
    n_pool = state_pool.shape[0]
    assert n_pool == 1 and g_mix.shape[0] == 2 and w_q.shape[0] == 1, "one pooling layer then one attention layer"

    row = lambda a: a.reshape(1, -1).astype(F32)
    wg = w_gate.astype(BF16)
    wu = w_up.astype(BF16)
    wd = w_down.astype(BF16)
    pw = pool_w[0].astype(BF16)
    wk = w_k.astype(BF16)
    wv = w_v.astype(BF16)
    wq = w_q[0].astype(BF16)
    wo = w_o[0].astype(BF16)
    wf = jnp.pad(w_f, ((0, 0), (0, LANES - H))).astype(BF16)
    head_of = jnp.arange(A, dtype=jnp.int32) // HEAD_DIM
    avg = jnp.where(head_of[:, None] == head_of[None, :], 1.0 / HEAD_DIM, 0.0).astype(BF16)
    gk = row(jnp.tile(g_knorm, H))
    gqn = row(jnp.tile(g_qnorm[0], H))

    tile_p = 512
    tile_s = Ss

    def stream(x, prev, pos0, tile):
        x2, st = _layer0(x, prev, row(g_mix[0]), pw, row(pool_scale[0]), row(g_ffn[0]), wg[0], wu[0], wd[0],
                         pos0=pos0, tile=tile)
        k, v, logf, qb, kb, vt = _proj(x2, row(g_kv), row(g_mix[1]), wk, wv, wf, row(b_f), wq, gk, gqn, avg,
                                       tile=tile)
        return x2, st[:, HIST_ROWS - POOL_HIST:], k, v, logf, qb, kb, vt

    zero_prev = jnp.zeros((B, HIST_ROWS, D), F32)
    xp2, st_p, k_p, v_p, logf_p, qb_p, kb_p, vt_p = stream(x_prompt, zero_prev, 0, tile_p)
    prev_s = jnp.pad(state_pool[0], ((0, 0), (HIST_ROWS - POOL_HIST, 0), (0, 0)))
    xs2, st_s, k_s, v_s, logf_s, qb_s, kb_s, vt_s = stream(x_sample, prev_s, past, tile_s)

    dq_p, dk_p = _pair_layouts(_cumsum_log2(logf_p, tile=512))
    o_p = _flash(qb_p, kb_p, vt_p, dq_p, dk_p, tq=512, tk=512, q_off=0)
    y_p = _layer1(xp2, o_p, wo, row(g_ffn[1]), wg[1], wu[1], wd[1], tile=tile_p)

    sk = past + Ss
    sk_pad = -(-sk // MXU_DIM) * MXU_DIM
    n_pad = sk_pad - sk
    k_all = jnp.pad(jnp.concatenate([cache_k.reshape(Bs, past, A).astype(BF16), kb_s], axis=1),
                    ((0, 0), (0, n_pad), (0, 0)))
    vt_cache = jnp.transpose(cache_v.reshape(Bs, past, A), (0, 2, 1)).astype(BF16)
    vt_all = jnp.pad(jnp.concatenate([vt_cache, vt_s], axis=2), ((0, 0), (0, 0), (0, n_pad)))
    logf_all = jnp.pad(jnp.concatenate([cache_logf.astype(F32), logf_s], axis=1), ((0, 0), (0, n_pad), (0, 0)))
    dq_all, dk_s = _pair_layouts(_cumsum_log2(logf_all, tile=sk_pad))
    dq_s = dq_all[:, :, :, past:sk]
    o_s = _flash(qb_s, k_all, vt_all, dq_s, dk_s, tq=Ss, tk=sk_pad, q_off=past)
    y_s = _layer1(xs2, o_s, wo, row(g_ffn[1]), wg[1], wu[1], wd[1], tile=tile_s)

    return (y_p, y_s, st_p[None], st_s[None],
            k_p.reshape(B, S, H, HEAD_DIM), v_p.reshape(B, S, H, HEAD_DIM), logf_p,
            k_s.reshape(Bs, Ss, H, HEAD_DIM), v_s.reshape(Bs, Ss, H, HEAD_DIM), logf_s)
```

```python
import functools
import math

import jax
import jax.numpy as jnp
from jax import lax
from jax.experimental import pallas as pl
from jax.experimental.pallas import tpu as pltpu

POOL_WINDOWS = (2, 4, 8, 16)
POOL_HIST = max(POOL_WINDOWS) - 1
HIST_ROWS = 16
HEAD_DIM = 64
LANES = 128
SUBLANES = 8
MXU_DIM = 256
KEY_SUB = 128
ROW_CHUNK = 16
RMS_EPS = 1e-6
NEG_INF = -1e30
LOG2E = math.log2(math.e)
VMEM_LIMIT_BYTES = 56 * 1024 * 1024

F32 = jnp.float32
BF16 = jnp.bfloat16


def _const_spec(shape):
    return pl.BlockSpec(shape, lambda *_: (0,) * len(shape), pipeline_mode=pl.Buffered(1))


def _rms_scale(x):
    return lax.rsqrt(jnp.mean(x * x, axis=-1, keepdims=True) + RMS_EPS)


def _swiglu(x1, g_ffn, wg_ref, wu_ref, wd_ref):
    h = (x1 * _rms_scale(x1) * g_ffn).astype(BF16)
    gate = jnp.dot(h, wg_ref[...], preferred_element_type=F32)
    up = jnp.dot(h, wu_ref[...], preferred_element_type=F32)
    act = (gate * jax.nn.sigmoid(gate) * up).astype(BF16)
    return x1 + jnp.dot(act, wd_ref[...], preferred_element_type=F32)


def _layer0_kernel(x_ref, prev_ref, gmix_ref, poolw_ref, pscale_ref, gffn_ref, wg_ref, wu_ref, wd_ref,
                   y_ref, state_ref, ext_ref, *, pos0, tile):
    j = pl.program_id(1)
    x = x_ref[0]
    h = x * _rms_scale(x) * gmix_ref[...]

    @pl.when(j == 0)
    def _():
        ext_ref[0:HIST_ROWS, :] = prev_ref[0]

    @pl.when(j > 0)
    def _():
        ext_ref[0:HIST_ROWS, :] = ext_ref[tile:tile + HIST_ROWS, :]

    ext_ref[HIST_ROWS:HIST_ROWS + tile, :] = h
    state_ref[0] = ext_ref[tile:tile + HIST_ROWS, :]

    pos = pos0 + j * tile + lax.broadcasted_iota(jnp.int32, (tile, 1), 0)
    width = x.shape[-1] // len(POOL_WINDOWS)
    outs = []
    for g, w in enumerate(POOL_WINDOWS):
        c0 = g * width
        hg = ext_ref[HIST_ROWS:HIST_ROWS + tile, c0:c0 + width]
        win = hg
        for k in range(1, w):
            win = win + ext_ref[HIST_ROWS - k:HIST_ROWS - k + tile, c0:c0 + width]
        cnt = jnp.minimum(pos + 1, w).astype(F32)
        pooled = win / cnt - hg
        outs.append(jnp.dot(pooled.astype(BF16), poolw_ref[g], preferred_element_type=F32))
    x1 = x + jnp.concatenate(outs, axis=-1) * pscale_ref[...]
    y_ref[0] = _swiglu(x1, gffn_ref[...], wg_ref, wu_ref, wd_ref)


def _layer0(x, prev, g_mix, pool_w, pool_scale, g_ffn, wg, wu, wd, *, pos0, tile):
    B, S, D = x.shape
    F = wg.shape[1]
    G, W = pool_w.shape[0], pool_w.shape[1]
    kern = functools.partial(_layer0_kernel, pos0=pos0, tile=tile)
    return pl.pallas_call(
        kern,
        grid=(B, S // tile),
        in_specs=[
            pl.BlockSpec((1, tile, D), lambda b, j: (b, j, 0)),
            pl.BlockSpec((1, HIST_ROWS, D), lambda b, j: (b, 0, 0)),
            _const_spec((1, D)),
            _const_spec((G, W, W)),
            _const_spec((1, D)),
            _const_spec((1, D)),
            _const_spec((D, F)),
            _const_spec((D, F)),
            _const_spec((F, D)),
        ],
        out_specs=[
            pl.BlockSpec((1, tile, D), lambda b, j: (b, j, 0)),
            pl.BlockSpec((1, HIST_ROWS, D), lambda b, j: (b, 0, 0)),
        ],
        out_shape=[
            jax.ShapeDtypeStruct((B, S, D), F32),
            jax.ShapeDtypeStruct((B, HIST_ROWS, D), F32),
        ],
        scratch_shapes=[pltpu.VMEM((HIST_ROWS + tile, D), F32)],
        compiler_params=pltpu.CompilerParams(
            dimension_semantics=("arbitrary", "arbitrary"), vmem_limit_bytes=VMEM_LIMIT_BYTES),
        name="layer0_pool_ffn",
    )(x, prev, g_mix, pool_w, pool_scale, g_ffn, wg, wu, wd)


def _proj_kernel(x_ref, gkv_ref, gq_ref, wk_ref, wv_ref, wf_ref, bf_ref, wq_ref, gk_ref, gqn_ref, avg_ref,
                 k_ref, v_ref, logf_ref, qb_ref, kb_ref, vt_ref, *, n_heads):
    x = x_ref[0]
    xn = x * _rms_scale(x)
    hkv = (xn * gkv_ref[...]).astype(BF16)
    hq = (xn * gq_ref[...]).astype(BF16)

    def head_norm(y, g):
        ms = jnp.dot((y * y).astype(BF16), avg_ref[...], preferred_element_type=F32)
        return y * lax.rsqrt(ms + RMS_EPS) * g

    k = head_norm(jnp.dot(hkv, wk_ref[...], preferred_element_type=F32), gk_ref[...])
    k_ref[0] = k
    kb_ref[0] = k.astype(BF16)
    v = jnp.dot(hkv, wv_ref[...], preferred_element_type=F32)
    v_ref[0] = v
    vt_ref[0] = v.T.astype(BF16)
    z = jnp.dot(hkv, wf_ref[...], preferred_element_type=F32)[:, :n_heads] + bf_ref[...]
    logf_ref[0] = -(jnp.maximum(-z, 0.0) + jnp.log1p(jnp.exp(-jnp.abs(z))))
    q = head_norm(jnp.dot(hq, wq_ref[...], preferred_element_type=F32), gqn_ref[...])
    qb_ref[0] = (q * (HEAD_DIM ** -0.5 * LOG2E)).astype(BF16)


def _proj(x, g_kv, g_q, wk, wv, wf, bf, wq, gk, gqn, avg, *, tile):
    B, S, D = x.shape
    A = wk.shape[1]
    H = bf.shape[1]
    tok = lambda b, j: (b, j, 0)
    return pl.pallas_call(
        functools.partial(_proj_kernel, n_heads=H),
        grid=(B, S // tile),
        in_specs=[
            pl.BlockSpec((1, tile, D), tok),
            _const_spec((1, D)), _const_spec((1, D)),
            _const_spec((D, A)), _const_spec((D, A)), _const_spec((D, LANES)), _const_spec((1, H)),
            _const_spec((D, A)), _const_spec((1, A)), _const_spec((1, A)), _const_spec((A, A)),
        ],
        out_specs=[
            pl.BlockSpec((1, tile, A), tok), pl.BlockSpec((1, tile, A), tok), pl.BlockSpec((1, tile, H), tok),
            pl.BlockSpec((1, tile, A), tok), pl.BlockSpec((1, tile, A), tok),
            pl.BlockSpec((1, A, tile), lambda b, j: (b, 0, j)),
        ],
        out_shape=[
            jax.ShapeDtypeStruct((B, S, A), F32), jax.ShapeDtypeStruct((B, S, A), F32),
            jax.ShapeDtypeStruct((B, S, H), F32),
            jax.ShapeDtypeStruct((B, S, A), BF16), jax.ShapeDtypeStruct((B, S, A), BF16),
            jax.ShapeDtypeStruct((B, A, S), BF16),
        ],
        compiler_params=pltpu.CompilerParams(
            dimension_semantics=("arbitrary", "arbitrary"), vmem_limit_bytes=VMEM_LIMIT_BYTES),
        name="kvq_proj",
    )(x, g_kv, g_q, wk, wv, wf, bf, wq, gk, gqn, avg)


def _cumsum_kernel(x_ref, o_ref, carry_ref, *, tile):
    j = pl.program_id(1)

    @pl.when(j == 0)
    def _():
        carry_ref[...] = jnp.zeros_like(carry_ref)

    x = x_ref[0]
    hi = x.astype(BF16)
    r1 = x - hi.astype(F32)
    mid = r1.astype(BF16)
    lo = (r1 - mid.astype(F32)).astype(BF16)
    row = lax.broadcasted_iota(jnp.int32, (tile, tile), 0)
    col = lax.broadcasted_iota(jnp.int32, (tile, tile), 1)
    tri = (col <= row).astype(BF16)
    cs = (jnp.dot(tri, hi, preferred_element_type=F32) + jnp.dot(tri, mid, preferred_element_type=F32)
          + jnp.dot(tri, lo, preferred_element_type=F32)) + carry_ref[...]
    o_ref[0] = cs * LOG2E
    carry_ref[...] = cs[tile - 1:tile, :]


def _cumsum_log2(x, *, tile):
    B, S, H = x.shape
    return pl.pallas_call(
        functools.partial(_cumsum_kernel, tile=tile),
        grid=(B, S // tile),
        in_specs=[pl.BlockSpec((1, tile, H), lambda b, j: (b, j, 0))],
        out_specs=pl.BlockSpec((1, tile, H), lambda b, j: (b, j, 0)),
        out_shape=jax.ShapeDtypeStruct((B, S, H), F32),
        scratch_shapes=[pltpu.VMEM((1, H), F32)],
        compiler_params=pltpu.CompilerParams(dimension_semantics=("arbitrary", "arbitrary")),
        name="logf_cumsum",
    )(x)


def _flash_kernel(q_ref, k_ref, vt_ref, dq_ref, dk_ref, o_ref, qh_ref, u_ref, p_ref, mblk_ref, m_ref, l_ref,
                  acc_ref, *, tq, tk, q_off):
    qi = pl.program_id(2)
    q_start = q_off + qi * tq
    n_full = q_start // tk
    n_sub = tk // KEY_SUB

    q = q_ref[0]
    lane = lax.broadcasted_iota(jnp.int32, (1, LANES), 1)
    zero = jnp.zeros_like(q)
    qh_ref[0] = jnp.where(lane < HEAD_DIM, q, zero)
    qh_ref[1] = jnp.where(lane >= HEAD_DIM, q, zero)
    dq = dq_ref[0, 0]

    m_ref[...] = jnp.full_like(m_ref, NEG_INF)
    l_ref[...] = jnp.zeros_like(l_ref)
    acc_ref[...] = jnp.zeros_like(acc_ref)

    def key_start(j):
        return pl.multiple_of(j * tk, tk)

    def score_phase(k_start, h, masked):
        m_acc = jnp.full((ROW_CHUNK, tq), NEG_INF, F32)
        if masked:
            q_pos = q_start + lax.broadcasted_iota(jnp.int32, (ROW_CHUNK, tq), 1)
            k_iota = lax.broadcasted_iota(jnp.int32, (ROW_CHUNK, tq), 0)
        for sb in range(n_sub):
            r0 = sb * KEY_SUB
            kb = k_ref[0, pl.ds(k_start + r0, KEY_SUB), :]
            dk = dk_ref[0, 0, pl.ds(k_start + r0, KEY_SUB), :]
            s = lax.dot_general(kb, qh_ref[h], (((1,), (1,)), ((), ())), preferred_element_type=F32)
            for r in range(0, KEY_SUB, ROW_CHUNK):
                u = s[r:r + ROW_CHUNK, :] - dk[r:r + ROW_CHUNK, h:h + 1]
                if masked:
                    u = jnp.where(k_iota + (k_start + r0 + r) <= q_pos, u, NEG_INF)
                u_ref[h, r0 + r:r0 + r + ROW_CHUNK, :] = u
                m_acc = jnp.maximum(m_acc, u)
        mblk_ref[h, 0:1, :] = jnp.max(m_acc, axis=0, keepdims=True)

    def softmax_phase(h):
        dq_h = dq[h:h + 1, :]
        m_old = m_ref[h, 0:1, :]
        m_new = jnp.maximum(m_old, mblk_ref[h, 0:1, :] + dq_h)
        alpha = jnp.exp2(m_old - m_new)
        shift = jnp.broadcast_to(dq_h - m_new, (ROW_CHUNK, tq))
        for r in range(0, tk, ROW_CHUNK):
            p_ref[h, r:r + ROW_CHUNK, :] = jnp.exp2(u_ref[h, r:r + ROW_CHUNK, :] + shift).astype(BF16)
        m_ref[h, 0:1, :] = m_new
        return alpha

    def value_phase(k_start, h, alpha):
        rows = slice(h * HEAD_DIM, (h + 1) * HEAD_DIM)
        vt_ones = jnp.concatenate(
            [vt_ref[0, rows, pl.ds(k_start, tk)], jnp.ones((ROW_CHUNK, tk), BF16)], axis=0)
        pv = jnp.dot(vt_ones, p_ref[h], preferred_element_type=F32)
        acc_ref[rows, :] = alpha * acc_ref[rows, :] + pv[:HEAD_DIM, :]
        l_ref[h, 0:1, :] = alpha * l_ref[h, 0:1, :] + pv[HEAD_DIM:HEAD_DIM + 1, :]

    def pipelined_block(j, next_masked):
        k_start = key_start(j)
        score_phase(k_start, 1, False)
        value_phase(k_start, 0, softmax_phase(0))
        score_phase(key_start(j + 1), 0, next_masked)
        value_phase(k_start, 1, softmax_phase(1))

    def full_block(j, carry):
        pipelined_block(j, False)
        return carry

    @pl.when(n_full == 0)
    def _():
        score_phase(key_start(0), 0, True)

    @pl.when(n_full > 0)
    def _():
        score_phase(key_start(0), 0, False)

    lax.fori_loop(0, n_full - 1, full_block, 0)

    @pl.when(n_full > 0)
    def _():
        pipelined_block(n_full - 1, True)

    k_last = key_start(n_full)
    score_phase(k_last, 1, True)
    value_phase(k_last, 0, softmax_phase(0))
    value_phase(k_last, 1, softmax_phase(1))

    out_t = jnp.concatenate(
        [acc_ref[h * HEAD_DIM:(h + 1) * HEAD_DIM, :] / l_ref[h, 0:1, :] for h in range(2)], axis=0)
    o_ref[0] = out_t.T.astype(o_ref.dtype)


def _flash(q, k, vt, dq, dk, *, tq, tk, q_off):
    B, Sq, A = q.shape
    Sk = k.shape[1]
    HP = A // LANES
    assert Sq % tq == 0 and Sk % tk == 0
    assert (tq == tk and q_off % tk == 0) or (Sk == tk and q_off + Sq <= Sk)
    return pl.pallas_call(
        functools.partial(_flash_kernel, tq=tq, tk=tk, q_off=q_off),
        grid=(B, HP, Sq // tq),
        in_specs=[
            pl.BlockSpec((1, tq, LANES), lambda b, hp, i: (b, i, hp)),
            pl.BlockSpec((1, Sk, LANES), lambda b, hp, i: (b, 0, hp)),
            pl.BlockSpec((1, LANES, Sk), lambda b, hp, i: (b, hp, 0)),
            pl.BlockSpec((1, 1, 2, tq), lambda b, hp, i: (b, hp, 0, i)),
            pl.BlockSpec((1, 1, Sk, 2), lambda b, hp, i: (b, hp, 0, 0)),
        ],
        out_specs=pl.BlockSpec((1, tq, LANES), lambda b, hp, i: (b, i, hp)),
        out_shape=jax.ShapeDtypeStruct((B, Sq, A), BF16),
        scratch_shapes=[
            pltpu.VMEM((2, tq, LANES), BF16), pltpu.VMEM((2, tk, tq), F32), pltpu.VMEM((2, tk, tq), BF16),
            pltpu.VMEM((2, SUBLANES, tq), F32), pltpu.VMEM((2, SUBLANES, tq), F32),
            pltpu.VMEM((2, SUBLANES, tq), F32), pltpu.VMEM((LANES, tq), F32)],
        compiler_params=pltpu.CompilerParams(
            dimension_semantics=("arbitrary", "arbitrary", "arbitrary"), vmem_limit_bytes=VMEM_LIMIT_BYTES),
        name="fox_flash",
    )(q, k, vt, dq, dk)


def _layer1_kernel(x_ref, o_ref, wo_ref, gffn_ref, wg_ref, wu_ref, wd_ref, y_ref):
    x1 = x_ref[0] + jnp.dot(o_ref[0], wo_ref[...], preferred_element_type=F32)
    y_ref[0] = _swiglu(x1, gffn_ref[...], wg_ref, wu_ref, wd_ref)


def _layer1(x, o, wo, g_ffn, wg, wu, wd, *, tile):
    B, S, D = x.shape
    A = o.shape[-1]
    F = wg.shape[1]
    tok = lambda b, j: (b, j, 0)
    return pl.pallas_call(
        _layer1_kernel,
        grid=(B, S // tile),
        in_specs=[
            pl.BlockSpec((1, tile, D), tok), pl.BlockSpec((1, tile, A), tok),
            _const_spec((A, D)), _const_spec((1, D)),
            _const_spec((D, F)), _const_spec((D, F)), _const_spec((F, D)),
        ],
        out_specs=pl.BlockSpec((1, tile, D), tok),
        out_shape=jax.ShapeDtypeStruct((B, S, D), F32),
        compiler_params=pltpu.CompilerParams(
            dimension_semantics=("arbitrary", "arbitrary"), vmem_limit_bytes=VMEM_LIMIT_BYTES),
        name="layer1_out_ffn",
    )(x, o, wo, g_ffn, wg, wu, wd)


def _pair_layouts(d):
    B, S, H = d.shape
    d4 = d.reshape(B, S, H // 2, 2)
    return jnp.transpose(d4, (0, 2, 3, 1)), jnp.transpose(d4, (0, 2, 1, 3))


def kernel(x_prompt, x_sample, state_pool, cache_k, cache_v, cache_logf, g_mix, g_ffn, pool_w, pool_scale,
           g_kv, w_k, w_v, w_f, b_f, g_knorm, w_q, g_qnorm, w_o, w_gate, w_up, w_down):
    B, S, D = x_prompt.shape
    Bs, Ss, _ = x_sample.shape
    past = cache_k.shape[1]
    H = cache_k.shape[2]
    A = H * HEAD_DIM
    n_pool = state_pool.shape[0]
    assert n_pool == 1 and g_mix.shape[0] == 2 and w_q.shape[0] == 1, "one pooling layer then one attention layer"

    row = lambda a: a.reshape(1, -1).astype(F32)
    wg = w_gate.astype(BF16)
    wu = w_up.astype(BF16)
    wd = w_down.astype(BF16)
    pw = pool_w[0].astype(BF16)
    wk = w_k.astype(BF16)
    wv = w_v.astype(BF16)
    wq = w_q[0].astype(BF16)
    wo = w_o[0].astype(BF16)
    wf = jnp.pad(w_f, ((0, 0), (0, LANES - H))).astype(BF16)
    head_of = jnp.arange(A, dtype=jnp.int32) // HEAD_DIM
    avg = jnp.where(head_of[:, None] == head_of[None, :], 1.0 / HEAD_DIM, 0.0).astype(BF16)
    gk = row(jnp.tile(g_knorm, H))
    gqn = row(jnp.tile(g_qnorm[0], H))

    tile_p = 512
    tile_s = Ss

    def stream(x, prev, pos0, tile):
        x2, st = _layer0(x, prev, row(g_mix[0]), pw, row(pool_scale[0]), row(g_ffn[0]), wg[0], wu[0], wd[0],
                         pos0=pos0, tile=tile)
        k, v, logf, qb, kb, vt = _proj(x2, row(g_kv), row(g_mix[1]), wk, wv, wf, row(b_f), wq, gk, gqn, avg,
                                       tile=tile)
        return x2, st[:, HIST_ROWS - POOL_HIST:], k, v, logf, qb, kb, vt

    zero_prev = jnp.zeros((B, HIST_ROWS, D), F32)
    xp2, st_p, k_p, v_p, logf_p, qb_p, kb_p, vt_p = stream(x_prompt, zero_prev, 0, tile_p)
    prev_s = jnp.pad(state_pool[0], ((0, 0), (HIST_ROWS - POOL_HIST, 0), (0, 0)))
    xs2, st_s, k_s, v_s, logf_s, qb_s, kb_s, vt_s = stream(x_sample, prev_s, past, tile_s)

    dq_p, dk_p = _pair_layouts(_cumsum_log2(logf_p, tile=512))
    o_p = _flash(qb_p, kb_p, vt_p, dq_p, dk_p, tq=512, tk=512, q_off=0)
    y_p = _layer1(xp2, o_p, wo, row(g_ffn[1]), wg[1], wu[1], wd[1], tile=tile_p)

    sk = past + Ss
    sk_pad = -(-sk // MXU_DIM) * MXU_DIM
    n_pad = sk_pad - sk
    k_all = jnp.pad(jnp.concatenate([cache_k.reshape(Bs, past, A).astype(BF16), kb_s], axis=1),
                    ((0, 0), (0, n_pad), (0, 0)))
    vt_cache = jnp.transpose(cache_v.reshape(Bs, past, A), (0, 2, 1)).astype(BF16)
    vt_all = jnp.pad(jnp.concatenate([vt_cache, vt_s], axis=2), ((0, 0), (0, 0), (0, n_pad)))
    logf_all = jnp.pad(jnp.concatenate([cache_logf.astype(F32), logf_s], axis=1), ((0, 0), (0, n_pad), (0, 0)))
    dq_all, dk_s = _pair_layouts(_cumsum_log2(logf_all, tile=sk_pad))
    dq_s = dq_all[:, :, :, past:sk]
    o_s = _flash(qb_s, k_all, vt_all, dq_s, dk_s, tq=Ss, tk=sk_pad, q_off=past)
    y_s = _layer1(xs2, o_s, wo, row(g_ffn[1]), wg[1], wu[1], wd[1], tile=tile_s)

    return (y_p, y_s, st_p[None], st_s[None],
            k_p.reshape(B, S, H, HEAD_DIM), v_p.reshape(B, S, H, HEAD_DIM), logf_p,
            k_s.reshape(Bs, Ss, H, HEAD_DIM), v_s.reshape(Bs, Ss, H, HEAD_DIM), logf_s)
```

```python
import functools
import math

import jax
import jax.numpy as jnp
from jax import lax
from jax.experimental import pallas as pl
from jax.experimental.pallas import tpu as pltpu

POOL_WINDOWS = (2, 4, 8, 16)
POOL_HIST = max(POOL_WINDOWS) - 1
HIST_ROWS = 16
HEAD_DIM = 64
LANES = 128
SUBLANES = 8
MXU_DIM = 256
KEY_SUB = 128
ROW_CHUNK = 16
RMS_EPS = 1e-6
NEG_INF = -1e30
LOG2E = math.log2(math.e)
VMEM_LIMIT_BYTES = 56 * 1024 * 1024

F32 = jnp.float32
BF16 = jnp.bfloat16


def _const_spec(shape):
    return pl.BlockSpec(shape, lambda *_: (0,) * len(shape), pipeline_mode=pl.Buffered(1))


def _rms_scale(x):
    return lax.rsqrt(jnp.mean(x * x, axis=-1, keepdims=True) + RMS_EPS)


def _swiglu(x1, g_ffn, wg_ref, wu_ref, wd_ref):
    h = (x1 * _rms_scale(x1) * g_ffn).astype(BF16)
    gate = jnp.dot(h, wg_ref[...], preferred_element_type=F32)
    up = jnp.dot(h, wu_ref[...], preferred_element_type=F32)
    act = (gate * jax.nn.sigmoid(gate) * up).astype(BF16)
    return x1 + jnp.dot(act, wd_ref[...], preferred_element_type=F32)


def _layer0_kernel(x_ref, prev_ref, gmix_ref, poolw_ref, pscale_ref, gffn_ref, wg_ref, wu_ref, wd_ref,
                   y_ref, state_ref, ext_ref, *, pos0, tile):
    j = pl.program_id(1)
    x = x_ref[0]
    h = x * _rms_scale(x) * gmix_ref[...]

    @pl.when(j == 0)
    def _():
        ext_ref[0:HIST_ROWS, :] = prev_ref[0]

    @pl.when(j > 0)
    def _():
        ext_ref[0:HIST_ROWS, :] = ext_ref[tile:tile + HIST_ROWS, :]

    ext_ref[HIST_ROWS:HIST_ROWS + tile, :] = h
    state_ref[0] = ext_ref[tile:tile + HIST_ROWS, :]

    pos = pos0 + j * tile + lax.broadcasted_iota(jnp.int32, (tile, 1), 0)
    width = x.shape[-1] // len(POOL_WINDOWS)
    outs = []
    for g, w in enumerate(POOL_WINDOWS):
        c0 = g * width
        hg = ext_ref[HIST_ROWS:HIST_ROWS + tile, c0:c0 + width]
        win = hg
        for k in range(1, w):
            win = win + ext_ref[HIST_ROWS - k:HIST_ROWS - k + tile, c0:c0 + width]
        cnt = jnp.minimum(pos + 1, w).astype(F32)
        pooled = win / cnt - hg
        outs.append(jnp.dot(pooled.astype(BF16), poolw_ref[g], preferred_element_type=F32))
    x1 = x + jnp.concatenate(outs, axis=-1) * pscale_ref[...]
    y_ref[0] = _swiglu(x1, gffn_ref[...], wg_ref, wu_ref, wd_ref)


def _layer0(x, prev, g_mix, pool_w, pool_scale, g_ffn, wg, wu, wd, *, pos0, tile):
    B, S, D = x.shape
    F = wg.shape[1]
    G, W = pool_w.shape[0], pool_w.shape[1]
    kern = functools.partial(_layer0_kernel, pos0=pos0, tile=tile)
    return pl.pallas_call(
        kern,
        grid=(B, S // tile),
        in_specs=[
            pl.BlockSpec((1, tile, D), lambda b, j: (b, j, 0)),
            pl.BlockSpec((1, HIST_ROWS, D), lambda b, j: (b, 0, 0)),
            _const_spec((1, D)),
            _const_spec((G, W, W)),
            _const_spec((1, D)),
            _const_spec((1, D)),
            _const_spec((D, F)),
            _const_spec((D, F)),
            _const_spec((F, D)),
        ],
        out_specs=[
            pl.BlockSpec((1, tile, D), lambda b, j: (b, j, 0)),
            pl.BlockSpec((1, HIST_ROWS, D), lambda b, j: (b, 0, 0)),
        ],
        out_shape=[
            jax.ShapeDtypeStruct((B, S, D), F32),
            jax.ShapeDtypeStruct((B, HIST_ROWS, D), F32),
        ],
        scratch_shapes=[pltpu.VMEM((HIST_ROWS + tile, D), F32)],
        compiler_params=pltpu.CompilerParams(
            dimension_semantics=("arbitrary", "arbitrary"), vmem_limit_bytes=VMEM_LIMIT_BYTES),
        name="layer0_pool_ffn",
    )(x, prev, g_mix, pool_w, pool_scale, g_ffn, wg, wu, wd)


def _store_pairs(ref, y):
    for hp in range(y.shape[1] // LANES):
        ref[0, hp] = y[:, hp * LANES:(hp + 1) * LANES]


def _proj_kernel(x_ref, gkv_ref, gq_ref, wk_ref, wv_ref, wf_ref, bf_ref, wq_ref, gk_ref, gqn_ref, avg_ref,
                 k_ref, v_ref, logf_ref, qb_ref, kb_ref, vt_ref, *, n_heads):
    x = x_ref[0]
    xn = x * _rms_scale(x)
    hkv = (xn * gkv_ref[...]).astype(BF16)
    hq = (xn * gq_ref[...]).astype(BF16)

    def head_norm(y, g):
        ms = jnp.dot((y * y).astype(BF16), avg_ref[...], preferred_element_type=F32)
        return y * lax.rsqrt(ms + RMS_EPS) * g

    k = head_norm(jnp.dot(hkv, wk_ref[...], preferred_element_type=F32), gk_ref[...])
    k_ref[0] = k
    _store_pairs(kb_ref, k.astype(BF16))
    v = jnp.dot(hkv, wv_ref[...], preferred_element_type=F32)
    v_ref[0] = v
    vt_ref[0] = v.T.astype(BF16)
    z = jnp.dot(hkv, wf_ref[...], preferred_element_type=F32)[:, :n_heads] + bf_ref[...]
    logf_ref[0] = -(jnp.maximum(-z, 0.0) + jnp.log1p(jnp.exp(-jnp.abs(z))))
    q = head_norm(jnp.dot(hq, wq_ref[...], preferred_element_type=F32), gqn_ref[...])
    _store_pairs(qb_ref, (q * (HEAD_DIM ** -0.5 * LOG2E)).astype(BF16))


def _proj(x, g_kv, g_q, wk, wv, wf, bf, wq, gk, gqn, avg, *, tile):
    B, S, D = x.shape
    A = wk.shape[1]
    H = bf.shape[1]
    tok = lambda b, j: (b, j, 0)
    return pl.pallas_call(
        functools.partial(_proj_kernel, n_heads=H),
        grid=(B, S // tile),
        in_specs=[
            pl.BlockSpec((1, tile, D), tok),
            _const_spec((1, D)), _const_spec((1, D)),
            _const_spec((D, A)), _const_spec((D, A)), _const_spec((D, LANES)), _const_spec((1, H)),
            _const_spec((D, A)), _const_spec((1, A)), _const_spec((1, A)), _const_spec((A, A)),
        ],
        out_specs=[
            pl.BlockSpec((1, tile, A), tok), pl.BlockSpec((1, tile, A), tok), pl.BlockSpec((1, tile, H), tok),
            pl.BlockSpec((1, A // LANES, tile, LANES), lambda b, j: (b, 0, j, 0)),
            pl.BlockSpec((1, A // LANES, tile, LANES), lambda b, j: (b, 0, j, 0)),
            pl.BlockSpec((1, A, tile), lambda b, j: (b, 0, j)),
        ],
        out_shape=[
            jax.ShapeDtypeStruct((B, S, A), F32), jax.ShapeDtypeStruct((B, S, A), F32),
            jax.ShapeDtypeStruct((B, S, H), F32),
            jax.ShapeDtypeStruct((B, A // LANES, S, LANES), BF16),
            jax.ShapeDtypeStruct((B, A // LANES, S, LANES), BF16),
            jax.ShapeDtypeStruct((B, A, S), BF16),
        ],
        compiler_params=pltpu.CompilerParams(
            dimension_semantics=("arbitrary", "arbitrary"), vmem_limit_bytes=VMEM_LIMIT_BYTES),
        name="kvq_proj",
    )(x, g_kv, g_q, wk, wv, wf, bf, wq, gk, gqn, avg)


def _cumsum_kernel(x_ref, o_ref, carry_ref, *, tile):
    j = pl.program_id(1)

    @pl.when(j == 0)
    def _():
        carry_ref[...] = jnp.zeros_like(carry_ref)

    x = x_ref[0]
    hi = x.astype(BF16)
    r1 = x - hi.astype(F32)
    mid = r1.astype(BF16)
    lo = (r1 - mid.astype(F32)).astype(BF16)
    row = lax.broadcasted_iota(jnp.int32, (tile, tile), 0)
    col = lax.broadcasted_iota(jnp.int32, (tile, tile), 1)
    tri = (col <= row).astype(BF16)
    cs = (jnp.dot(tri, hi, preferred_element_type=F32) + jnp.dot(tri, mid, preferred_element_type=F32)
          + jnp.dot(tri, lo, preferred_element_type=F32)) + carry_ref[...]
    o_ref[0] = cs * LOG2E
    carry_ref[...] = cs[tile - 1:tile, :]


def _cumsum_log2(x, *, tile):
    B, S, H = x.shape
    return pl.pallas_call(
        functools.partial(_cumsum_kernel, tile=tile),
        grid=(B, S // tile),
        in_specs=[pl.BlockSpec((1, tile, H), lambda b, j: (b, j, 0))],
        out_specs=pl.BlockSpec((1, tile, H), lambda b, j: (b, j, 0)),
        out_shape=jax.ShapeDtypeStruct((B, S, H), F32),
        scratch_shapes=[pltpu.VMEM((1, H), F32)],
        compiler_params=pltpu.CompilerParams(dimension_semantics=("arbitrary", "arbitrary")),
        name="logf_cumsum",
    )(x)


def _flash_kernel(q_ref, k_ref, vt_ref, dq_ref, dk_ref, o_ref, qh_ref, u_ref, p_ref, mblk_ref, m_ref, l_ref,
                  acc_ref, *, tq, tk, q_off):
    qi = pl.program_id(2)
    q_start = q_off + qi * tq
    n_full = q_start // tk
    n_sub = tk // KEY_SUB

    q = q_ref[0, 0]
    lane = lax.broadcasted_iota(jnp.int32, (1, LANES), 1)
    zero = jnp.zeros_like(q)
    qh_ref[0] = jnp.where(lane < HEAD_DIM, q, zero)
    qh_ref[1] = jnp.where(lane >= HEAD_DIM, q, zero)
    dq = dq_ref[0, 0]

    m_ref[...] = jnp.full_like(m_ref, NEG_INF)
    l_ref[...] = jnp.zeros_like(l_ref)
    acc_ref[...] = jnp.zeros_like(acc_ref)

    def key_start(j):
        return pl.multiple_of(j * tk, tk)

    def score_phase(k_start, h, masked):
        m_acc = jnp.full((ROW_CHUNK, tq), NEG_INF, F32)
        if masked:
            q_pos = q_start + lax.broadcasted_iota(jnp.int32, (ROW_CHUNK, tq), 1)
            k_iota = lax.broadcasted_iota(jnp.int32, (ROW_CHUNK, tq), 0)
        for sb in range(n_sub):
            r0 = sb * KEY_SUB
            kb = k_ref[0, 0, pl.ds(k_start + r0, KEY_SUB), :]
            dk_rows = dk_ref[0, 0, :, pl.ds(k_start + r0, KEY_SUB)]
            dk = jnp.concatenate([dk_rows, jnp.zeros((KEY_SUB - SUBLANES, KEY_SUB), F32)], axis=0).T
            s = lax.dot_general(kb, qh_ref[h], (((1,), (1,)), ((), ())), preferred_element_type=F32)
            for r in range(0, KEY_SUB, ROW_CHUNK):
                u = s[r:r + ROW_CHUNK, :] - dk[r:r + ROW_CHUNK, h:h + 1]
                if masked:
                    u = jnp.where(k_iota + (k_start + r0 + r) <= q_pos, u, NEG_INF)
                u_ref[h, r0 + r:r0 + r + ROW_CHUNK, :] = u
                m_acc = jnp.maximum(m_acc, u)
        mblk_ref[h, 0:1, :] = jnp.max(m_acc, axis=0, keepdims=True)

    def softmax_phase(h):
        dq_h = dq[h:h + 1, :]
        m_old = m_ref[h, 0:1, :]
        m_new = jnp.maximum(m_old, mblk_ref[h, 0:1, :] + dq_h)
        alpha = jnp.exp2(m_old - m_new)
        shift = jnp.broadcast_to(dq_h - m_new, (ROW_CHUNK, tq))
        for r in range(0, tk, ROW_CHUNK):
            p_ref[h, r:r + ROW_CHUNK, :] = jnp.exp2(u_ref[h, r:r + ROW_CHUNK, :] + shift).astype(BF16)
        m_ref[h, 0:1, :] = m_new
        return alpha

    def value_phase(k_start, h, alpha):
        rows = slice(h * HEAD_DIM, (h + 1) * HEAD_DIM)
        vt_ones = jnp.concatenate(
            [vt_ref[0, rows, pl.ds(k_start, tk)], jnp.ones((ROW_CHUNK, tk), BF16)], axis=0)
        pv = jnp.dot(vt_ones, p_ref[h], preferred_element_type=F32)
        acc_ref[rows, :] = alpha * acc_ref[rows, :] + pv[:HEAD_DIM, :]
        l_ref[h, 0:1, :] = alpha * l_ref[h, 0:1, :] + pv[HEAD_DIM:HEAD_DIM + 1, :]

    def pipelined_block(j, next_masked):
        k_start = key_start(j)
        score_phase(k_start, 1, False)
        value_phase(k_start, 0, softmax_phase(0))
        score_phase(key_start(j + 1), 0, next_masked)
        value_phase(k_start, 1, softmax_phase(1))

    def full_block(j, carry):
        pipelined_block(j, False)
        return carry

    @pl.when(n_full == 0)
    def _():
        score_phase(key_start(0), 0, True)

    @pl.when(n_full > 0)
    def _():
        score_phase(key_start(0), 0, False)

    lax.fori_loop(0, n_full - 1, full_block, 0)

    @pl.when(n_full > 0)
    def _():
        pipelined_block(n_full - 1, True)

    k_last = key_start(n_full)
    score_phase(k_last, 1, True)
    value_phase(k_last, 0, softmax_phase(0))
    value_phase(k_last, 1, softmax_phase(1))

    out_t = jnp.concatenate(
        [acc_ref[h * HEAD_DIM:(h + 1) * HEAD_DIM, :] / l_ref[h, 0:1, :] for h in range(2)], axis=0)
    o_ref[0, 0] = out_t.T.astype(o_ref.dtype)


def _flash(q, k, vt, dq, dk, *, tq, tk, q_off):
    B, HP, Sq, _ = q.shape
    Sk = k.shape[2]
    assert Sq % tq == 0 and Sk % tk == 0
    assert (tq == tk and q_off % tk == 0) or (Sk == tk and q_off + Sq <= Sk)
    return pl.pallas_call(
        functools.partial(_flash_kernel, tq=tq, tk=tk, q_off=q_off),
        grid=(B, HP, Sq // tq),
        in_specs=[
            pl.BlockSpec((1, 1, tq, LANES), lambda b, hp, i: (b, hp, i, 0)),
            pl.BlockSpec((1, 1, Sk, LANES), lambda b, hp, i: (b, hp, 0, 0)),
            pl.BlockSpec((1, LANES, Sk), lambda b, hp, i: (b, hp, 0)),
            pl.BlockSpec((1, 1, SUBLANES, tq), lambda b, hp, i: (b, hp, 0, i)),
            pl.BlockSpec((1, 1, SUBLANES, Sk), lambda b, hp, i: (b, hp, 0, 0)),
        ],
        out_specs=pl.BlockSpec((1, 1, tq, LANES), lambda b, hp, i: (b, hp, i, 0)),
        out_shape=jax.ShapeDtypeStruct((B, HP, Sq, LANES), BF16),
        scratch_shapes=[
            pltpu.VMEM((2, tq, LANES), BF16), pltpu.VMEM((2, tk, tq), F32), pltpu.VMEM((2, tk, tq), BF16),
            pltpu.VMEM((2, SUBLANES, tq), F32), pltpu.VMEM((2, SUBLANES, tq), F32),
            pltpu.VMEM((2, SUBLANES, tq), F32), pltpu.VMEM((LANES, tq), F32)],
        compiler_params=pltpu.CompilerParams(
            dimension_semantics=("arbitrary", "arbitrary", "arbitrary"), vmem_limit_bytes=VMEM_LIMIT_BYTES),
        name="fox_flash",
    )(q, k, vt, dq, dk)


def _layer1_kernel(x_ref, o_ref, wo_ref, gffn_ref, wg_ref, wu_ref, wd_ref, y_ref):
    o = jnp.concatenate([o_ref[0, hp] for hp in range(o_ref.shape[1])], axis=-1)
    x1 = x_ref[0] + jnp.dot(o, wo_ref[...], preferred_element_type=F32)
    y_ref[0] = _swiglu(x1, gffn_ref[...], wg_ref, wu_ref, wd_ref)


def _layer1(x, o, wo, g_ffn, wg, wu, wd, *, tile):
    B, S, D = x.shape
    HP = o.shape[1]
    A = HP * LANES
    F = wg.shape[1]
    tok = lambda b, j: (b, j, 0)
    return pl.pallas_call(
        _layer1_kernel,
        grid=(B, S // tile),
        in_specs=[
            pl.BlockSpec((1, tile, D), tok), pl.BlockSpec((1, HP, tile, LANES), lambda b, j: (b, 0, j, 0)),
            _const_spec((A, D)), _const_spec((1, D)),
            _const_spec((D, F)), _const_spec((D, F)), _const_spec((F, D)),
        ],
        out_specs=pl.BlockSpec((1, tile, D), tok),
        out_shape=jax.ShapeDtypeStruct((B, S, D), F32),
        compiler_params=pltpu.CompilerParams(
            dimension_semantics=("arbitrary", "arbitrary"), vmem_limit_bytes=VMEM_LIMIT_BYTES),
        name="layer1_out_ffn",
    )(x, o, wo, g_ffn, wg, wu, wd)


def _pair_rows(d):
    B, S, H = d.shape
    rows = jnp.transpose(d.reshape(B, S, H // 2, 2), (0, 2, 3, 1))
    return jnp.pad(rows, ((0, 0), (0, 0), (0, SUBLANES - 2), (0, 0)))


def kernel(x_prompt, x_sample, state_pool, cache_k, cache_v, cache_logf, g_mix, g_ffn, pool_w, pool_scale,
           g_kv, w_k, w_v, w_f, b_f, g_knorm, w_q, g_qnorm, w_o, w_gate, w_up, w_down):
    B, S, D = x_prompt.shape
    Bs, Ss, _ = x_sample.shape
    past = cache_k.shape[1]
    H = cache_k.shape[2]
    A = H * HEAD_DIM
    n_pool = state_pool.shape[0]
    assert n_pool == 1 and g_mix.shape[0] == 2 and w_q.shape[0] == 1, "one pooling layer then one attention layer"

    row = lambda a: a.reshape(1, -1).astype(F32)
    wg = w_gate.astype(BF16)
    wu = w_up.astype(BF16)
    wd = w_down.astype(BF16)
    pw = pool_w[0].astype(BF16)
    wk = w_k.astype(BF16)
    wv = w_v.astype(BF16)
    wq = w_q[0].astype(BF16)
    wo = w_o[0].astype(BF16)
    wf = jnp.pad(w_f, ((0, 0), (0, LANES - H))).astype(BF16)
    head_of = jnp.arange(A, dtype=jnp.int32) // HEAD_DIM
    avg = jnp.where(head_of[:, None] == head_of[None, :], 1.0 / HEAD_DIM, 0.0).astype(BF16)
    gk = row(jnp.tile(g_knorm, H))
    gqn = row(jnp.tile(g_qnorm[0], H))

    tile_p = 512
    tile_s = Ss

    def stream(x, prev, pos0, tile):
        x2, st = _layer0(x, prev, row(g_mix[0]), pw, row(pool_scale[0]), row(g_ffn[0]), wg[0], wu[0], wd[0],
                         pos0=pos0, tile=tile)
        k, v, logf, qb, kb, vt = _proj(x2, row(g_kv), row(g_mix[1]), wk, wv, wf, row(b_f), wq, gk, gqn, avg,
                                       tile=tile)
        return x2, st[:, HIST_ROWS - POOL_HIST:], k, v, logf, qb, kb, vt

    zero_prev = jnp.zeros((B, HIST_ROWS, D), F32)
    xp2, st_p, k_p, v_p, logf_p, qb_p, kb_p, vt_p = stream(x_prompt, zero_prev, 0, tile_p)
    prev_s = jnp.pad(state_pool[0], ((0, 0), (HIST_ROWS - POOL_HIST, 0), (0, 0)))
    xs2, st_s, k_s, v_s, logf_s, qb_s, kb_s, vt_s = stream(x_sample, prev_s, past, tile_s)

    d_p = _pair_rows(_cumsum_log2(logf_p, tile=512))
    o_p = _flash(qb_p, kb_p, vt_p, d_p, d_p, tq=512, tk=512, q_off=0)
    y_p = _layer1(xp2, o_p, wo, row(g_ffn[1]), wg[1], wu[1], wd[1], tile=tile_p)

    sk = past + Ss
    sk_pad = -(-sk // MXU_DIM) * MXU_DIM
    n_pad = sk_pad - sk
    k_cache = jnp.transpose(cache_k.reshape(Bs, past, A // LANES, LANES), (0, 2, 1, 3)).astype(BF16)
    k_all = jnp.pad(jnp.concatenate([k_cache, kb_s], axis=2), ((0, 0), (0, 0), (0, n_pad), (0, 0)))
    vt_cache = jnp.transpose(cache_v.reshape(Bs, past, A), (0, 2, 1)).astype(BF16)
    vt_all = jnp.pad(jnp.concatenate([vt_cache, vt_s], axis=2), ((0, 0), (0, 0), (0, n_pad)))
    logf_all = jnp.pad(jnp.concatenate([cache_logf.astype(F32), logf_s], axis=1), ((0, 0), (0, n_pad), (0, 0)))
    d_all = _pair_rows(_cumsum_log2(logf_all, tile=sk_pad))
    o_s = _flash(qb_s, k_all, vt_all, d_all[:, :, :, past:sk], d_all, tq=Ss, tk=sk_pad, q_off=past)
    y_s = _layer1(xs2, o_s, wo, row(g_ffn[1]), wg[1], wu[1], wd[1], tile=tile_s)

    return (y_p, y_s, st_p[None], st_s[None],
            k_p.reshape(B, S, H, HEAD_DIM), v_p.reshape(B, S, H, HEAD_DIM), logf_p,
            k_s.reshape(Bs, Ss, H, HEAD_DIM), v_s.reshape(Bs, Ss, H, HEAD_DIM), logf_s)
```

```python
import functools
import math

import jax
import jax.numpy as jnp
from jax import lax
from jax.experimental import pallas as pl
from jax.experimental.pallas import tpu as pltpu

POOL_WINDOWS = (2, 4, 8, 16)
POOL_HIST = max(POOL_WINDOWS) - 1
HIST_ROWS = 16
HEAD_DIM = 64
LANES = 128
SUBLANES = 8
MXU_DIM = 256
KEY_SUB = 128
ROW_CHUNK = 16
RMS_EPS = 1e-6
NEG_INF = -1e30
LOG2E = math.log2(math.e)
VMEM_LIMIT_BYTES = 56 * 1024 * 1024

F32 = jnp.float32
BF16 = jnp.bfloat16


def _const_spec(shape):
    return pl.BlockSpec(shape, lambda *_: (0,) * len(shape), pipeline_mode=pl.Buffered(1))


def _rms_scale(x):
    return lax.rsqrt(jnp.mean(x * x, axis=-1, keepdims=True) + RMS_EPS)


def _swiglu(x1, g_ffn, wg_ref, wu_ref, wd_ref):
    h = (x1 * _rms_scale(x1) * g_ffn).astype(BF16)
    gate = jnp.dot(h, wg_ref[...], preferred_element_type=F32)
    up = jnp.dot(h, wu_ref[...], preferred_element_type=F32)
    act = (gate * jax.nn.sigmoid(gate) * up).astype(BF16)
    return x1 + jnp.dot(act, wd_ref[...], preferred_element_type=F32)


def _layer0_kernel(x_ref, prev_ref, gmix_ref, poolw_ref, pscale_ref, gffn_ref, wg_ref, wu_ref, wd_ref,
                   y_ref, state_ref, ext_ref, *, pos0, tile):
    j = pl.program_id(1)
    x = x_ref[0]
    h = x * _rms_scale(x) * gmix_ref[...]

    @pl.when(j == 0)
    def _():
        ext_ref[0:HIST_ROWS, :] = prev_ref[0]

    @pl.when(j > 0)
    def _():
        ext_ref[0:HIST_ROWS, :] = ext_ref[tile:tile + HIST_ROWS, :]

    ext_ref[HIST_ROWS:HIST_ROWS + tile, :] = h
    state_ref[0] = ext_ref[tile:tile + HIST_ROWS, :]

    pos = pos0 + j * tile + lax.broadcasted_iota(jnp.int32, (tile, 1), 0)
    width = x.shape[-1] // len(POOL_WINDOWS)
    outs = []
    for g, w in enumerate(POOL_WINDOWS):
        c0 = g * width
        hg = ext_ref[HIST_ROWS:HIST_ROWS + tile, c0:c0 + width]
        win = hg
        for k in range(1, w):
            win = win + ext_ref[HIST_ROWS - k:HIST_ROWS - k + tile, c0:c0 + width]
        cnt = jnp.minimum(pos + 1, w).astype(F32)
        pooled = win / cnt - hg
        outs.append(jnp.dot(pooled.astype(BF16), poolw_ref[g], preferred_element_type=F32))
    x1 = x + jnp.concatenate(outs, axis=-1) * pscale_ref[...]
    y_ref[0] = _swiglu(x1, gffn_ref[...], wg_ref, wu_ref, wd_ref)


def _layer0(x, prev, g_mix, pool_w, pool_scale, g_ffn, wg, wu, wd, *, pos0, tile):
    B, S, D = x.shape
    F = wg.shape[1]
    G, W = pool_w.shape[0], pool_w.shape[1]
    kern = functools.partial(_layer0_kernel, pos0=pos0, tile=tile)
    return pl.pallas_call(
        kern,
        grid=(B, S // tile),
        in_specs=[
            pl.BlockSpec((1, tile, D), lambda b, j: (b, j, 0)),
            pl.BlockSpec((1, HIST_ROWS, D), lambda b, j: (b, 0, 0)),
            _const_spec((1, D)),
            _const_spec((G, W, W)),
            _const_spec((1, D)),
            _const_spec((1, D)),
            _const_spec((D, F)),
            _const_spec((D, F)),
            _const_spec((F, D)),
        ],
        out_specs=[
            pl.BlockSpec((1, tile, D), lambda b, j: (b, j, 0)),
            pl.BlockSpec((1, HIST_ROWS, D), lambda b, j: (b, 0, 0)),
        ],
        out_shape=[
            jax.ShapeDtypeStruct((B, S, D), F32),
            jax.ShapeDtypeStruct((B, HIST_ROWS, D), F32),
        ],
        scratch_shapes=[pltpu.VMEM((HIST_ROWS + tile, D), F32)],
        compiler_params=pltpu.CompilerParams(
            dimension_semantics=("arbitrary", "arbitrary"), vmem_limit_bytes=VMEM_LIMIT_BYTES),
        name="layer0_pool_ffn",
    )(x, prev, g_mix, pool_w, pool_scale, g_ffn, wg, wu, wd)


def _store_pairs(ref, y):
    for hp in range(y.shape[1] // LANES):
        ref[0, hp] = y[:, hp * LANES:(hp + 1) * LANES]


def _proj_kernel(x_ref, gkv_ref, gq_ref, wk_ref, wv_ref, wf_ref, bf_ref, wq_ref, gk_ref, gqn_ref, hsum_ref,
                 hspread_ref, k_ref, v_ref, logf_ref, qb_ref, kb_ref, vt_ref, *, n_heads):
    x = x_ref[0]
    xn = x * _rms_scale(x)
    hkv = (xn * gkv_ref[...]).astype(BF16)
    hq = (xn * gq_ref[...]).astype(BF16)

    def head_norm(y, g):
        ms = jnp.dot((y * y).astype(BF16), hsum_ref[...], preferred_element_type=F32)
        r = lax.rsqrt(ms + RMS_EPS)
        r_hi = r.astype(BF16)
        r_lo = (r - r_hi.astype(F32)).astype(BF16)
        scale = jnp.dot(jnp.concatenate([r_hi, r_lo], axis=-1), hspread_ref[...], preferred_element_type=F32)
        return y * scale * g

    k = head_norm(jnp.dot(hkv, wk_ref[...], preferred_element_type=F32), gk_ref[...])
    k_ref[0] = k
    _store_pairs(kb_ref, k.astype(BF16))
    v = jnp.dot(hkv, wv_ref[...], preferred_element_type=F32)
    v_ref[0] = v
    vt_ref[0] = v.T.astype(BF16)
    z = jnp.dot(hkv, wf_ref[...], preferred_element_type=F32)[:, :n_heads] + bf_ref[...]
    logf_ref[0] = -(jnp.maximum(-z, 0.0) + jnp.log1p(jnp.exp(-jnp.abs(z))))
    q = head_norm(jnp.dot(hq, wq_ref[...], preferred_element_type=F32), gqn_ref[...])
    _store_pairs(qb_ref, (q * (HEAD_DIM ** -0.5 * LOG2E)).astype(BF16))


def _proj(x, g_kv, g_q, wk, wv, wf, bf, wq, gk, gqn, head_sum, head_spread, *, tile):
    B, S, D = x.shape
    A = wk.shape[1]
    H = bf.shape[1]
    tok = lambda b, j: (b, j, 0)
    return pl.pallas_call(
        functools.partial(_proj_kernel, n_heads=H),
        grid=(B, S // tile),
        in_specs=[
            pl.BlockSpec((1, tile, D), tok),
            _const_spec((1, D)), _const_spec((1, D)),
            _const_spec((D, A)), _const_spec((D, A)), _const_spec((D, LANES)), _const_spec((1, H)),
            _const_spec((D, A)), _const_spec((1, A)), _const_spec((1, A)),
            _const_spec((A, LANES)), _const_spec((2 * LANES, A)),
        ],
        out_specs=[
            pl.BlockSpec((1, tile, A), tok), pl.BlockSpec((1, tile, A), tok), pl.BlockSpec((1, tile, H), tok),
            pl.BlockSpec((1, A // LANES, tile, LANES), lambda b, j: (b, 0, j, 0)),
            pl.BlockSpec((1, A // LANES, tile, LANES), lambda b, j: (b, 0, j, 0)),
            pl.BlockSpec((1, A, tile), lambda b, j: (b, 0, j)),
        ],
        out_shape=[
            jax.ShapeDtypeStruct((B, S, A), F32), jax.ShapeDtypeStruct((B, S, A), F32),
            jax.ShapeDtypeStruct((B, S, H), F32),
            jax.ShapeDtypeStruct((B, A // LANES, S, LANES), BF16),
            jax.ShapeDtypeStruct((B, A // LANES, S, LANES), BF16),
            jax.ShapeDtypeStruct((B, A, S), BF16),
        ],
        compiler_params=pltpu.CompilerParams(
            dimension_semantics=("arbitrary", "arbitrary"), vmem_limit_bytes=VMEM_LIMIT_BYTES),
        name="kvq_proj",
    )(x, g_kv, g_q, wk, wv, wf, bf, wq, gk, gqn, head_sum, head_spread)


def _cumsum_kernel(x_ref, o_ref, carry_ref, *, tile):
    j = pl.program_id(1)

    @pl.when(j == 0)
    def _():
        carry_ref[...] = jnp.zeros_like(carry_ref)

    x = x_ref[0]
    hi = x.astype(BF16)
    r1 = x - hi.astype(F32)
    mid = r1.astype(BF16)
    lo = (r1 - mid.astype(F32)).astype(BF16)
    row = lax.broadcasted_iota(jnp.int32, (tile, tile), 0)
    col = lax.broadcasted_iota(jnp.int32, (tile, tile), 1)
    tri = (col <= row).astype(BF16)
    cs = (jnp.dot(tri, hi, preferred_element_type=F32) + jnp.dot(tri, mid, preferred_element_type=F32)
          + jnp.dot(tri, lo, preferred_element_type=F32)) + carry_ref[...]
    o_ref[0] = cs * LOG2E
    carry_ref[...] = cs[tile - 1:tile, :]


def _cumsum_log2(x, *, tile):
    B, S, H = x.shape
    return pl.pallas_call(
        functools.partial(_cumsum_kernel, tile=tile),
        grid=(B, S // tile),
        in_specs=[pl.BlockSpec((1, tile, H), lambda b, j: (b, j, 0))],
        out_specs=pl.BlockSpec((1, tile, H), lambda b, j: (b, j, 0)),
        out_shape=jax.ShapeDtypeStruct((B, S, H), F32),
        scratch_shapes=[pltpu.VMEM((1, H), F32)],
        compiler_params=pltpu.CompilerParams(dimension_semantics=("arbitrary", "arbitrary")),
        name="logf_cumsum",
    )(x)


def _flash_kernel(q_ref, k_ref, vt_ref, dq_ref, dk_ref, o_ref, qh_ref, u_ref, p_ref, mblk_ref, m_ref, l_ref,
                  acc_ref, *, tq, tk, q_off):
    qi = pl.program_id(2)
    q_start = q_off + qi * tq
    n_full = q_start // tk
    n_sub = tk // KEY_SUB

    q = q_ref[0, 0]
    lane = lax.broadcasted_iota(jnp.int32, (1, LANES), 1)
    zero = jnp.zeros_like(q)
    qh_ref[0] = jnp.where(lane < HEAD_DIM, q, zero)
    qh_ref[1] = jnp.where(lane >= HEAD_DIM, q, zero)
    dq = dq_ref[0, 0]

    m_ref[...] = jnp.full_like(m_ref, NEG_INF)
    l_ref[...] = jnp.zeros_like(l_ref)
    acc_ref[...] = jnp.zeros_like(acc_ref)

    def key_start(j):
        return pl.multiple_of(j * tk, tk)

    rel = 0 if tq == tk else q_off
    col_w = min(LANES, tq)

    def first_col(sb):
        return max(0, -((rel + col_w - 1 - sb * KEY_SUB) // col_w)) * col_w

    def score_phase(k_start, h, masked):
        m_acc = jnp.full((ROW_CHUNK, tq), NEG_INF, F32)
        for sb in range(n_sub):
            r0 = sb * KEY_SUB
            c0 = first_col(sb) if masked else 0
            if c0 >= tq:
                continue
            kb = k_ref[0, 0, pl.ds(k_start + r0, KEY_SUB), :]
            dk_rows = dk_ref[0, 0, :, pl.ds(k_start + r0, KEY_SUB)]
            dk = jnp.concatenate([dk_rows, jnp.zeros((KEY_SUB - SUBLANES, KEY_SUB), F32)], axis=0).T
            s = lax.dot_general(kb, qh_ref[h, c0:, :], (((1,), (1,)), ((), ())), preferred_element_type=F32)
            if masked:
                q_rel = rel + c0 + lax.broadcasted_iota(jnp.int32, (ROW_CHUNK, tq - c0), 1)
                k_iota = lax.broadcasted_iota(jnp.int32, (ROW_CHUNK, tq - c0), 0)
            for r in range(0, KEY_SUB, ROW_CHUNK):
                u = s[r:r + ROW_CHUNK, :] - dk[r:r + ROW_CHUNK, h:h + 1]
                if masked:
                    u = jnp.where(k_iota + (r0 + r) <= q_rel, u, NEG_INF)
                u_ref[h, r0 + r:r0 + r + ROW_CHUNK, c0:] = u
                m_new_cols = jnp.maximum(m_acc[:, c0:], u)
                m_acc = m_new_cols if c0 == 0 else jnp.concatenate([m_acc[:, :c0], m_new_cols], axis=1)
        mblk_ref[h, 0:1, :] = jnp.max(m_acc, axis=0, keepdims=True)

    def softmax_phase(h, masked=False):
        dq_h = dq[h:h + 1, :]
        m_old = m_ref[h, 0:1, :]
        m_new = jnp.maximum(m_old, mblk_ref[h, 0:1, :] + dq_h)
        alpha = jnp.exp2(m_old - m_new)
        shift = jnp.broadcast_to(dq_h - m_new, (ROW_CHUNK, tq))
        for r in range(0, tk, ROW_CHUNK):
            c0 = min(first_col(r // KEY_SUB), tq) if masked else 0
            if c0 > 0:
                p_ref[h, r:r + ROW_CHUNK, :c0] = jnp.zeros((ROW_CHUNK, c0), BF16)
            if c0 < tq:
                p_ref[h, r:r + ROW_CHUNK, c0:] = jnp.exp2(
                    u_ref[h, r:r + ROW_CHUNK, c0:] + shift[:, c0:]).astype(BF16)
        m_ref[h, 0:1, :] = m_new
        return alpha

    def value_phase(k_start, h, alpha):
        rows = slice(h * HEAD_DIM, (h + 1) * HEAD_DIM)
        vt_ones = jnp.concatenate(
            [vt_ref[0, rows, pl.ds(k_start, tk)], jnp.ones((ROW_CHUNK, tk), BF16)], axis=0)
        pv = jnp.dot(vt_ones, p_ref[h], preferred_element_type=F32)
        acc_ref[rows, :] = alpha * acc_ref[rows, :] + pv[:HEAD_DIM, :]
        l_ref[h, 0:1, :] = alpha * l_ref[h, 0:1, :] + pv[HEAD_DIM:HEAD_DIM + 1, :]

    def pipelined_block(j, next_masked):
        k_start = key_start(j)
        score_phase(k_start, 1, False)
        value_phase(k_start, 0, softmax_phase(0))
        score_phase(key_start(j + 1), 0, next_masked)
        value_phase(k_start, 1, softmax_phase(1))

    def full_block(j, carry):
        pipelined_block(j, False)
        return carry

    @pl.when(n_full == 0)
    def _():
        score_phase(key_start(0), 0, True)

    @pl.when(n_full > 0)
    def _():
        score_phase(key_start(0), 0, False)

    lax.fori_loop(0, n_full - 1, full_block, 0)

    @pl.when(n_full > 0)
    def _():
        pipelined_block(n_full - 1, True)

    k_last = key_start(n_full)
    score_phase(k_last, 1, True)
    value_phase(k_last, 0, softmax_phase(0, masked=True))
    value_phase(k_last, 1, softmax_phase(1, masked=True))

    out_t = jnp.concatenate(
        [acc_ref[h * HEAD_DIM:(h + 1) * HEAD_DIM, :] / l_ref[h, 0:1, :] for h in range(2)], axis=0)
    o_ref[0, 0] = out_t.T.astype(o_ref.dtype)


def _flash(q, k, vt, dq, dk, *, tq, tk, q_off):
    B, HP, Sq, _ = q.shape
    Sk = k.shape[2]
    assert Sq % tq == 0 and Sk % tk == 0
    assert (tq == tk and q_off % tk == 0) or (Sk == tk and Sq == tq and q_off + Sq <= Sk)
    return pl.pallas_call(
        functools.partial(_flash_kernel, tq=tq, tk=tk, q_off=q_off),
        grid=(B, HP, Sq // tq),
        in_specs=[
            pl.BlockSpec((1, 1, tq, LANES), lambda b, hp, i: (b, hp, i, 0)),
            pl.BlockSpec((1, 1, Sk, LANES), lambda b, hp, i: (b, hp, 0, 0)),
            pl.BlockSpec((1, LANES, Sk), lambda b, hp, i: (b, hp, 0)),
            pl.BlockSpec((1, 1, SUBLANES, tq), lambda b, hp, i: (b, hp, 0, i)),
            pl.BlockSpec((1, 1, SUBLANES, Sk), lambda b, hp, i: (b, hp, 0, 0)),
        ],
        out_specs=pl.BlockSpec((1, 1, tq, LANES), lambda b, hp, i: (b, hp, i, 0)),
        out_shape=jax.ShapeDtypeStruct((B, HP, Sq, LANES), BF16),
        scratch_shapes=[
            pltpu.VMEM((2, tq, LANES), BF16), pltpu.VMEM((2, tk, tq), F32), pltpu.VMEM((2, tk, tq), BF16),
            pltpu.VMEM((2, SUBLANES, tq), F32), pltpu.VMEM((2, SUBLANES, tq), F32),
            pltpu.VMEM((2, SUBLANES, tq), F32), pltpu.VMEM((LANES, tq), F32)],
        compiler_params=pltpu.CompilerParams(
            dimension_semantics=("arbitrary", "arbitrary", "arbitrary"), vmem_limit_bytes=VMEM_LIMIT_BYTES),
        name="fox_flash",
    )(q, k, vt, dq, dk)


def _layer1_kernel(x_ref, o_ref, wo_ref, gffn_ref, wg_ref, wu_ref, wd_ref, y_ref):
    o = jnp.concatenate([o_ref[0, hp] for hp in range(o_ref.shape[1])], axis=-1)
    x1 = x_ref[0] + jnp.dot(o, wo_ref[...], preferred_element_type=F32)
    y_ref[0] = _swiglu(x1, gffn_ref[...], wg_ref, wu_ref, wd_ref)


def _layer1(x, o, wo, g_ffn, wg, wu, wd, *, tile):
    B, S, D = x.shape
    HP = o.shape[1]
    A = HP * LANES
    F = wg.shape[1]
    tok = lambda b, j: (b, j, 0)
    return pl.pallas_call(
        _layer1_kernel,
        grid=(B, S // tile),
        in_specs=[
            pl.BlockSpec((1, tile, D), tok), pl.BlockSpec((1, HP, tile, LANES), lambda b, j: (b, 0, j, 0)),
            _const_spec((A, D)), _const_spec((1, D)),
            _const_spec((D, F)), _const_spec((D, F)), _const_spec((F, D)),
        ],
        out_specs=pl.BlockSpec((1, tile, D), tok),
        out_shape=jax.ShapeDtypeStruct((B, S, D), F32),
        compiler_params=pltpu.CompilerParams(
            dimension_semantics=("arbitrary", "arbitrary"), vmem_limit_bytes=VMEM_LIMIT_BYTES),
        name="layer1_out_ffn",
    )(x, o, wo, g_ffn, wg, wu, wd)


def _pair_rows(d):
    B, S, H = d.shape
    rows = jnp.transpose(d.reshape(B, S, H // 2, 2), (0, 2, 3, 1))
    return jnp.pad(rows, ((0, 0), (0, 0), (0, SUBLANES - 2), (0, 0)))


def kernel(x_prompt, x_sample, state_pool, cache_k, cache_v, cache_logf, g_mix, g_ffn, pool_w, pool_scale,
           g_kv, w_k, w_v, w_f, b_f, g_knorm, w_q, g_qnorm, w_o, w_gate, w_up, w_down):
    B, S, D = x_prompt.shape
    Bs, Ss, _ = x_sample.shape
    past = cache_k.shape[1]
    H = cache_k.shape[2]
    A = H * HEAD_DIM
    n_pool = state_pool.shape[0]
    assert n_pool == 1 and g_mix.shape[0] == 2 and w_q.shape[0] == 1, "one pooling layer then one attention layer"

    row = lambda a: a.reshape(1, -1).astype(F32)
    wg = w_gate.astype(BF16)
    wu = w_up.astype(BF16)
    wd = w_down.astype(BF16)
    pw = pool_w[0].astype(BF16)
    wk = w_k.astype(BF16)
    wv = w_v.astype(BF16)
    wq = w_q[0].astype(BF16)
    wo = w_o[0].astype(BF16)
    wf = jnp.pad(w_f, ((0, 0), (0, LANES - H))).astype(BF16)
    head_of = jnp.arange(A, dtype=jnp.int32) // HEAD_DIM
    member = head_of[:, None] == jnp.arange(LANES, dtype=jnp.int32)[None, :]
    head_sum = jnp.where(member, 1.0 / HEAD_DIM, 0.0).astype(BF16)
    head_spread = jnp.tile(member.T.astype(BF16), (2, 1))
    gk = row(jnp.tile(g_knorm, H))
    gqn = row(jnp.tile(g_qnorm[0], H))

    tile_p = 512
    tile_s = Ss

    def stream(x, prev, pos0, tile):
        x2, st = _layer0(x, prev, row(g_mix[0]), pw, row(pool_scale[0]), row(g_ffn[0]), wg[0], wu[0], wd[0],
                         pos0=pos0, tile=tile)
        k, v, logf, qb, kb, vt = _proj(x2, row(g_kv), row(g_mix[1]), wk, wv, wf, row(b_f), wq, gk, gqn,
                                       head_sum, head_spread,
                                       tile=tile)
        return x2, st[:, HIST_ROWS - POOL_HIST:], k, v, logf, qb, kb, vt

    zero_prev = jnp.zeros((B, HIST_ROWS, D), F32)
    xp2, st_p, k_p, v_p, logf_p, qb_p, kb_p, vt_p = stream(x_prompt, zero_prev, 0, tile_p)
    prev_s = jnp.pad(state_pool[0], ((0, 0), (HIST_ROWS - POOL_HIST, 0), (0, 0)))
    xs2, st_s, k_s, v_s, logf_s, qb_s, kb_s, vt_s = stream(x_sample, prev_s, past, tile_s)

    d_p = _pair_rows(_cumsum_log2(logf_p, tile=512))
    o_p = _flash(qb_p, kb_p, vt_p, d_p, d_p, tq=512, tk=512, q_off=0)
    y_p = _layer1(xp2, o_p, wo, row(g_ffn[1]), wg[1], wu[1], wd[1], tile=tile_p)

    sk = past + Ss
    sk_pad = -(-sk // MXU_DIM) * MXU_DIM
    n_pad = sk_pad - sk
    k_cache = jnp.transpose(cache_k.reshape(Bs, past, A // LANES, LANES), (0, 2, 1, 3)).astype(BF16)
    k_all = jnp.pad(jnp.concatenate([k_cache, kb_s], axis=2), ((0, 0), (0, 0), (0, n_pad), (0, 0)))
    vt_cache = jnp.transpose(cache_v.reshape(Bs, past, A), (0, 2, 1)).astype(BF16)
    vt_all = jnp.pad(jnp.concatenate([vt_cache, vt_s], axis=2), ((0, 0), (0, 0), (0, n_pad)))
    logf_all = jnp.pad(jnp.concatenate([cache_logf.astype(F32), logf_s], axis=1), ((0, 0), (0, n_pad), (0, 0)))
    d_all = _pair_rows(_cumsum_log2(logf_all, tile=sk_pad))
    o_s = _flash(qb_s, k_all, vt_all, d_all[:, :, :, past:sk], d_all, tq=Ss, tk=sk_pad, q_off=past)
    y_s = _layer1(xs2, o_s, wo, row(g_ffn[1]), wg[1], wu[1], wd[1], tile=tile_s)

    return (y_p, y_s, st_p[None], st_s[None],
            k_p.reshape(B, S, H, HEAD_DIM), v_p.reshape(B, S, H, HEAD_DIM), logf_p,
            k_s.reshape(Bs, Ss, H, HEAD_DIM), v_s.reshape(Bs, Ss, H, HEAD_DIM), logf_s)
```

```python
import functools
import math

import jax
import jax.numpy as jnp
from jax import lax
from jax.experimental import pallas as pl
from jax.experimental.pallas import tpu as pltpu

POOL_WINDOWS = (2, 4, 8, 16)
POOL_HIST = max(POOL_WINDOWS) - 1
HIST_ROWS = 16
HEAD_DIM = 64
LANES = 128
SUBLANES = 8
MXU_DIM = 256
KEY_SUB = 128
ROW_CHUNK = 16
RMS_EPS = 1e-6
NEG_INF = -1e30
LOG2E = math.log2(math.e)
VMEM_LIMIT_BYTES = 56 * 1024 * 1024

F32 = jnp.float32
BF16 = jnp.bfloat16


def _const_spec(shape):
    return pl.BlockSpec(shape, lambda *_: (0,) * len(shape), pipeline_mode=pl.Buffered(1))


def _rms_scale(x):
    return lax.rsqrt(jnp.mean(x * x, axis=-1, keepdims=True) + RMS_EPS)


def _swiglu(x1, g_ffn, wg_ref, wu_ref, wd_ref):
    h = (x1 * _rms_scale(x1) * g_ffn).astype(BF16)
    gate = jnp.dot(h, wg_ref[...], preferred_element_type=F32)
    up = jnp.dot(h, wu_ref[...], preferred_element_type=F32)
    act = (gate * jax.nn.sigmoid(gate) * up).astype(BF16)
    return x1 + jnp.dot(act, wd_ref[...], preferred_element_type=F32)


def _layer0_kernel(x_ref, prev_ref, gmix_ref, poolw_ref, pscale_ref, gffn_ref, wg_ref, wu_ref, wd_ref,
                   y_ref, state_ref, ext_ref, *, pos0, tile):
    j = pl.program_id(1)
    x = x_ref[0]
    h = x * _rms_scale(x) * gmix_ref[...]

    @pl.when(j == 0)
    def _():
        ext_ref[0:HIST_ROWS, :] = prev_ref[0]

    @pl.when(j > 0)
    def _():
        ext_ref[0:HIST_ROWS, :] = ext_ref[tile:tile + HIST_ROWS, :]

    ext_ref[HIST_ROWS:HIST_ROWS + tile, :] = h
    state_ref[0] = ext_ref[tile:tile + HIST_ROWS, :]

    width = x.shape[-1] // len(POOL_WINDOWS)
    n_split = 2 if tile % (2 * HIST_ROWS) == 0 else 1
    rows = tile // n_split
    for part in range(n_split):
        t0 = part * rows
        e0 = HIST_ROWS + t0
        pos = pos0 + j * tile + t0 + lax.broadcasted_iota(jnp.int32, (rows, 1), 0)
        outs = []
        for g, w in enumerate(POOL_WINDOWS):
            c0 = g * width
            hg = ext_ref[e0:e0 + rows, c0:c0 + width]
            win = hg
            for k in range(1, w):
                win = win + ext_ref[e0 - k:e0 - k + rows, c0:c0 + width]
            cnt = jnp.minimum(pos + 1, w).astype(F32)
            pooled = win / cnt - hg
            outs.append(jnp.dot(pooled.astype(BF16), poolw_ref[g], preferred_element_type=F32))
        x1 = x_ref[0, t0:t0 + rows, :] + jnp.concatenate(outs, axis=-1) * pscale_ref[...]
        y_ref[0, t0:t0 + rows, :] = _swiglu(x1, gffn_ref[...], wg_ref, wu_ref, wd_ref)


def _layer0(x, prev, g_mix, pool_w, pool_scale, g_ffn, wg, wu, wd, *, pos0, tile):
    B, S, D = x.shape
    F = wg.shape[1]
    G, W = pool_w.shape[0], pool_w.shape[1]
    kern = functools.partial(_layer0_kernel, pos0=pos0, tile=tile)
    return pl.pallas_call(
        kern,
        grid=(B, S // tile),
        in_specs=[
            pl.BlockSpec((1, tile, D), lambda b, j: (b, j, 0)),
            pl.BlockSpec((1, HIST_ROWS, D), lambda b, j: (b, 0, 0)),
            _const_spec((1, D)),
            _const_spec((G, W, W)),
            _const_spec((1, D)),
            _const_spec((1, D)),
            _const_spec((D, F)),
            _const_spec((D, F)),
            _const_spec((F, D)),
        ],
        out_specs=[
            pl.BlockSpec((1, tile, D), lambda b, j: (b, j, 0)),
            pl.BlockSpec((1, HIST_ROWS, D), lambda b, j: (b, 0, 0)),
        ],
        out_shape=[
            jax.ShapeDtypeStruct((B, S, D), F32),
            jax.ShapeDtypeStruct((B, HIST_ROWS, D), F32),
        ],
        scratch_shapes=[pltpu.VMEM((HIST_ROWS + tile, D), F32)],
        compiler_params=pltpu.CompilerParams(
            dimension_semantics=("arbitrary", "arbitrary"), vmem_limit_bytes=VMEM_LIMIT_BYTES),
        name="layer0_pool_ffn",
    )(x, prev, g_mix, pool_w, pool_scale, g_ffn, wg, wu, wd)


def _store_pairs(ref, y):
    for hp in range(y.shape[1] // LANES):
        ref[0, hp] = y[:, hp * LANES:(hp + 1) * LANES]


def _proj_kernel(x_ref, gkv_ref, gq_ref, wk_ref, wv_ref, wf_ref, bf_ref, wq_ref, gk_ref, gqn_ref, hsum_ref,
                 hspread_ref, k_ref, v_ref, logf_ref, qb_ref, kb_ref, vt_ref, *, n_heads):
    x = x_ref[0]
    xn = x * _rms_scale(x)
    hkv = (xn * gkv_ref[...]).astype(BF16)
    hq = (xn * gq_ref[...]).astype(BF16)

    def head_norm(y, g):
        ms = jnp.dot((y * y).astype(BF16), hsum_ref[...], preferred_element_type=F32)
        r = lax.rsqrt(ms + RMS_EPS)
        r_hi = r.astype(BF16)
        r_lo = (r - r_hi.astype(F32)).astype(BF16)
        scale = jnp.dot(jnp.concatenate([r_hi, r_lo], axis=-1), hspread_ref[...], preferred_element_type=F32)
        return y * scale * g

    k = head_norm(jnp.dot(hkv, wk_ref[...], preferred_element_type=F32), gk_ref[...])
    k_ref[0] = k
    _store_pairs(kb_ref, k.astype(BF16))
    v = jnp.dot(hkv, wv_ref[...], preferred_element_type=F32)
    v_ref[0] = v
    vt_ref[0] = v.T.astype(BF16)
    z = jnp.dot(hkv, wf_ref[...], preferred_element_type=F32)[:, :n_heads] + bf_ref[...]
    logf_ref[0] = -(jnp.maximum(-z, 0.0) + jnp.log1p(jnp.exp(-jnp.abs(z))))
    q = head_norm(jnp.dot(hq, wq_ref[...], preferred_element_type=F32), gqn_ref[...])
    _store_pairs(qb_ref, (q * (HEAD_DIM ** -0.5 * LOG2E)).astype(BF16))


def _proj(x, g_kv, g_q, wk, wv, wf, bf, wq, gk, gqn, head_sum, head_spread, *, tile):
    B, S, D = x.shape
    A = wk.shape[1]
    H = bf.shape[1]
    tok = lambda b, j: (b, j, 0)
    return pl.pallas_call(
        functools.partial(_proj_kernel, n_heads=H),
        grid=(B, S // tile),
        in_specs=[
            pl.BlockSpec((1, tile, D), tok),
            _const_spec((1, D)), _const_spec((1, D)),
            _const_spec((D, A)), _const_spec((D, A)), _const_spec((D, LANES)), _const_spec((1, H)),
            _const_spec((D, A)), _const_spec((1, A)), _const_spec((1, A)),
            _const_spec((A, LANES)), _const_spec((2 * LANES, A)),
        ],
        out_specs=[
            pl.BlockSpec((1, tile, A), tok), pl.BlockSpec((1, tile, A), tok), pl.BlockSpec((1, tile, H), tok),
            pl.BlockSpec((1, A // LANES, tile, LANES), lambda b, j: (b, 0, j, 0)),
            pl.BlockSpec((1, A // LANES, tile, LANES), lambda b, j: (b, 0, j, 0)),
            pl.BlockSpec((1, A, tile), lambda b, j: (b, 0, j)),
        ],
        out_shape=[
            jax.ShapeDtypeStruct((B, S, A), F32), jax.ShapeDtypeStruct((B, S, A), F32),
            jax.ShapeDtypeStruct((B, S, H), F32),
            jax.ShapeDtypeStruct((B, A // LANES, S, LANES), BF16),
            jax.ShapeDtypeStruct((B, A // LANES, S, LANES), BF16),
            jax.ShapeDtypeStruct((B, A, S), BF16),
        ],
        compiler_params=pltpu.CompilerParams(
            dimension_semantics=("arbitrary", "arbitrary"), vmem_limit_bytes=VMEM_LIMIT_BYTES),
        name="kvq_proj",
    )(x, g_kv, g_q, wk, wv, wf, bf, wq, gk, gqn, head_sum, head_spread)


def _cumsum_kernel(x_ref, o_ref, carry_ref, *, tile):
    j = pl.program_id(1)

    @pl.when(j == 0)
    def _():
        carry_ref[...] = jnp.zeros_like(carry_ref)

    x = x_ref[0]
    hi = x.astype(BF16)
    r1 = x - hi.astype(F32)
    mid = r1.astype(BF16)
    lo = (r1 - mid.astype(F32)).astype(BF16)
    row = lax.broadcasted_iota(jnp.int32, (tile, tile), 0)
    col = lax.broadcasted_iota(jnp.int32, (tile, tile), 1)
    tri = (col <= row).astype(BF16)
    cs = (jnp.dot(tri, hi, preferred_element_type=F32) + jnp.dot(tri, mid, preferred_element_type=F32)
          + jnp.dot(tri, lo, preferred_element_type=F32)) + carry_ref[...]
    o_ref[0] = cs * LOG2E
    carry_ref[...] = cs[tile - 1:tile, :]


def _cumsum_log2(x, *, tile):
    B, S, H = x.shape
    return pl.pallas_call(
        functools.partial(_cumsum_kernel, tile=tile),
        grid=(B, S // tile),
        in_specs=[pl.BlockSpec((1, tile, H), lambda b, j: (b, j, 0))],
        out_specs=pl.BlockSpec((1, tile, H), lambda b, j: (b, j, 0)),
        out_shape=jax.ShapeDtypeStruct((B, S, H), F32),
        scratch_shapes=[pltpu.VMEM((1, H), F32)],
        compiler_params=pltpu.CompilerParams(dimension_semantics=("arbitrary", "arbitrary")),
        name="logf_cumsum",
    )(x)


def _flash_kernel(q_ref, k_ref, vt_ref, dq_ref, dk_ref, o_ref, qh_ref, u_ref, p_ref, mblk_ref, m_ref, l_ref,
                  acc_ref, *, tq, tk, q_off):
    qi = pl.program_id(2)
    q_start = q_off + qi * tq
    n_full = q_start // tk
    n_sub = tk // KEY_SUB

    q = q_ref[0, 0]
    lane = lax.broadcasted_iota(jnp.int32, (1, LANES), 1)
    zero = jnp.zeros_like(q)
    qh_ref[0] = jnp.where(lane < HEAD_DIM, q, zero)
    qh_ref[1] = jnp.where(lane >= HEAD_DIM, q, zero)
    dq = dq_ref[0, 0]

    m_ref[...] = jnp.full_like(m_ref, NEG_INF)
    l_ref[...] = jnp.zeros_like(l_ref)
    acc_ref[...] = jnp.zeros_like(acc_ref)

    def key_start(j):
        return pl.multiple_of(j * tk, tk)

    rel = 0 if tq == tk else q_off
    col_w = min(LANES, tq)

    def first_col(sb):
        return max(0, -((rel + col_w - 1 - sb * KEY_SUB) // col_w)) * col_w

    def score_phase(k_start, h, masked):
        m_acc = jnp.full((ROW_CHUNK, tq), NEG_INF, F32)
        for sb in range(n_sub):
            r0 = sb * KEY_SUB
            c0 = first_col(sb) if masked else 0
            if c0 >= tq:
                continue
            kb = k_ref[0, 0, pl.ds(k_start + r0, KEY_SUB), :]
            dk_rows = dk_ref[0, 0, :, pl.ds(k_start + r0, KEY_SUB)]
            dk = jnp.concatenate([dk_rows, jnp.zeros((KEY_SUB - SUBLANES, KEY_SUB), F32)], axis=0).T
            s = lax.dot_general(kb, qh_ref[h, c0:, :], (((1,), (1,)), ((), ())), preferred_element_type=F32)
            if masked:
                q_rel = rel + c0 + lax.broadcasted_iota(jnp.int32, (ROW_CHUNK, tq - c0), 1)
                k_iota = lax.broadcasted_iota(jnp.int32, (ROW_CHUNK, tq - c0), 0)
            for r in range(0, KEY_SUB, ROW_CHUNK):
                u = s[r:r + ROW_CHUNK, :] - dk[r:r + ROW_CHUNK, h:h + 1]
                if masked:
                    u = jnp.where(k_iota + (r0 + r) <= q_rel, u, NEG_INF)
                u_ref[h, r0 + r:r0 + r + ROW_CHUNK, c0:] = u
                m_new_cols = jnp.maximum(m_acc[:, c0:], u)
                m_acc = m_new_cols if c0 == 0 else jnp.concatenate([m_acc[:, :c0], m_new_cols], axis=1)
        mblk_ref[h, 0:1, :] = jnp.max(m_acc, axis=0, keepdims=True)

    def softmax_phase(h, masked=False):
        dq_h = dq[h:h + 1, :]
        m_old = m_ref[h, 0:1, :]
        m_new = jnp.maximum(m_old, mblk_ref[h, 0:1, :] + dq_h)
        alpha = jnp.exp2(m_old - m_new)
        shift = jnp.broadcast_to(dq_h - m_new, (ROW_CHUNK, tq))
        for r in range(0, tk, ROW_CHUNK):
            c0 = min(first_col(r // KEY_SUB), tq) if masked else 0
            if c0 > 0:
                p_ref[h, r:r + ROW_CHUNK, :c0] = jnp.zeros((ROW_CHUNK, c0), BF16)
            if c0 < tq:
                p_ref[h, r:r + ROW_CHUNK, c0:] = jnp.exp2(
                    u_ref[h, r:r + ROW_CHUNK, c0:] + shift[:, c0:]).astype(BF16)
        m_ref[h, 0:1, :] = m_new
        return alpha

    def value_phase(k_start, h, alpha):
        rows = slice(h * HEAD_DIM, (h + 1) * HEAD_DIM)
        vt_ones = jnp.concatenate(
            [vt_ref[0, rows, pl.ds(k_start, tk)], jnp.ones((ROW_CHUNK, tk), BF16)], axis=0)
        pv = jnp.dot(vt_ones, p_ref[h], preferred_element_type=F32)
        acc_ref[rows, :] = alpha * acc_ref[rows, :] + pv[:HEAD_DIM, :]
        l_ref[h, 0:1, :] = alpha * l_ref[h, 0:1, :] + pv[HEAD_DIM:HEAD_DIM + 1, :]

    def pipelined_block(j, next_masked):
        k_start = key_start(j)
        score_phase(k_start, 1, False)
        value_phase(k_start, 0, softmax_phase(0))
        score_phase(key_start(j + 1), 0, next_masked)
        value_phase(k_start, 1, softmax_phase(1))

    def full_block(j, carry):
        pipelined_block(j, False)
        return carry

    @pl.when(n_full == 0)
    def _():
        score_phase(key_start(0), 0, True)

    @pl.when(n_full > 0)
    def _():
        score_phase(key_start(0), 0, False)

    lax.fori_loop(0, n_full - 1, full_block, 0)

    @pl.when(n_full > 0)
    def _():
        pipelined_block(n_full - 1, True)

    k_last = key_start(n_full)
    score_phase(k_last, 1, True)
    value_phase(k_last, 0, softmax_phase(0, masked=True))
    value_phase(k_last, 1, softmax_phase(1, masked=True))

    out_t = jnp.concatenate(
        [acc_ref[h * HEAD_DIM:(h + 1) * HEAD_DIM, :] / l_ref[h, 0:1, :] for h in range(2)], axis=0)
    o_ref[0, 0] = out_t.T.astype(o_ref.dtype)


def _flash(q, k, vt, dq, dk, *, tq, tk, q_off):
    B, HP, Sq, _ = q.shape
    Sk = k.shape[2]
    assert Sq % tq == 0 and Sk % tk == 0
    assert (tq == tk and q_off % tk == 0) or (Sk == tk and Sq == tq and q_off + Sq <= Sk)
    return pl.pallas_call(
        functools.partial(_flash_kernel, tq=tq, tk=tk, q_off=q_off),
        grid=(B, HP, Sq // tq),
        in_specs=[
            pl.BlockSpec((1, 1, tq, LANES), lambda b, hp, i: (b, hp, i, 0)),
            pl.BlockSpec((1, 1, Sk, LANES), lambda b, hp, i: (b, hp, 0, 0)),
            pl.BlockSpec((1, LANES, Sk), lambda b, hp, i: (b, hp, 0)),
            pl.BlockSpec((1, 1, SUBLANES, tq), lambda b, hp, i: (b, hp, 0, i)),
            pl.BlockSpec((1, 1, SUBLANES, Sk), lambda b, hp, i: (b, hp, 0, 0)),
        ],
        out_specs=pl.BlockSpec((1, 1, tq, LANES), lambda b, hp, i: (b, hp, i, 0)),
        out_shape=jax.ShapeDtypeStruct((B, HP, Sq, LANES), BF16),
        scratch_shapes=[
            pltpu.VMEM((2, tq, LANES), BF16), pltpu.VMEM((2, tk, tq), F32), pltpu.VMEM((2, tk, tq), BF16),
            pltpu.VMEM((2, SUBLANES, tq), F32), pltpu.VMEM((2, SUBLANES, tq), F32),
            pltpu.VMEM((2, SUBLANES, tq), F32), pltpu.VMEM((LANES, tq), F32)],
        compiler_params=pltpu.CompilerParams(
            dimension_semantics=("arbitrary", "arbitrary", "arbitrary"), vmem_limit_bytes=VMEM_LIMIT_BYTES),
        name="fox_flash",
    )(q, k, vt, dq, dk)


def _layer1_kernel(x_ref, o_ref, wo_ref, gffn_ref, wg_ref, wu_ref, wd_ref, y_ref):
    o = jnp.concatenate([o_ref[0, hp] for hp in range(o_ref.shape[1])], axis=-1)
    x1 = x_ref[0] + jnp.dot(o, wo_ref[...], preferred_element_type=F32)
    y_ref[0] = _swiglu(x1, gffn_ref[...], wg_ref, wu_ref, wd_ref)


def _layer1(x, o, wo, g_ffn, wg, wu, wd, *, tile):
    B, S, D = x.shape
    HP = o.shape[1]
    A = HP * LANES
    F = wg.shape[1]
    tok = lambda b, j: (b, j, 0)
    return pl.pallas_call(
        _layer1_kernel,
        grid=(B, S // tile),
        in_specs=[
            pl.BlockSpec((1, tile, D), tok), pl.BlockSpec((1, HP, tile, LANES), lambda b, j: (b, 0, j, 0)),
            _const_spec((A, D)), _const_spec((1, D)),
            _const_spec((D, F)), _const_spec((D, F)), _const_spec((F, D)),
        ],
        out_specs=pl.BlockSpec((1, tile, D), tok),
        out_shape=jax.ShapeDtypeStruct((B, S, D), F32),
        compiler_params=pltpu.CompilerParams(
            dimension_semantics=("arbitrary", "arbitrary"), vmem_limit_bytes=VMEM_LIMIT_BYTES),
        name="layer1_out_ffn",
    )(x, o, wo, g_ffn, wg, wu, wd)


def _pair_rows(d):
    B, S, H = d.shape
    rows = jnp.transpose(d.reshape(B, S, H // 2, 2), (0, 2, 3, 1))
    return jnp.pad(rows, ((0, 0), (0, 0), (0, SUBLANES - 2), (0, 0)))


def kernel(x_prompt, x_sample, state_pool, cache_k, cache_v, cache_logf, g_mix, g_ffn, pool_w, pool_scale,
           g_kv, w_k, w_v, w_f, b_f, g_knorm, w_q, g_qnorm, w_o, w_gate, w_up, w_down):
    B, S, D = x_prompt.shape
    Bs, Ss, _ = x_sample.shape
    past = cache_k.shape[1]
    H = cache_k.shape[2]
    A = H * HEAD_DIM
    n_pool = state_pool.shape[0]
    assert n_pool == 1 and g_mix.shape[0] == 2 and w_q.shape[0] == 1, "one pooling layer then one attention layer"

    row = lambda a: a.reshape(1, -1).astype(F32)
    wg = w_gate.astype(BF16)
    wu = w_up.astype(BF16)
    wd = w_down.astype(BF16)
    pw = pool_w[0].astype(BF16)
    wk = w_k.astype(BF16)
    wv = w_v.astype(BF16)
    wq = w_q[0].astype(BF16)
    wo = w_o[0].astype(BF16)
    wf = jnp.pad(w_f, ((0, 0), (0, LANES - H))).astype(BF16)
    head_of = jnp.arange(A, dtype=jnp.int32) // HEAD_DIM
    member = head_of[:, None] == jnp.arange(LANES, dtype=jnp.int32)[None, :]
    head_sum = jnp.where(member, 1.0 / HEAD_DIM, 0.0).astype(BF16)
    head_spread = jnp.tile(member.T.astype(BF16), (2, 1))
    gk = row(jnp.tile(g_knorm, H))
    gqn = row(jnp.tile(g_qnorm[0], H))

    tile_p = 512
    tile_s = Ss

    def stream(x, prev, pos0, tile):
        x2, st = _layer0(x, prev, row(g_mix[0]), pw, row(pool_scale[0]), row(g_ffn[0]), wg[0], wu[0], wd[0],
                         pos0=pos0, tile=tile)
        k, v, logf, qb, kb, vt = _proj(x2, row(g_kv), row(g_mix[1]), wk, wv, wf, row(b_f), wq, gk, gqn,
                                       head_sum, head_spread,
                                       tile=tile)
        return x2, st[:, HIST_ROWS - POOL_HIST:], k, v, logf, qb, kb, vt

    zero_prev = jnp.zeros((B, HIST_ROWS, D), F32)
    xp2, st_p, k_p, v_p, logf_p, qb_p, kb_p, vt_p = stream(x_prompt, zero_prev, 0, tile_p)
    prev_s = jnp.pad(state_pool[0], ((0, 0), (HIST_ROWS - POOL_HIST, 0), (0, 0)))
    xs2, st_s, k_s, v_s, logf_s, qb_s, kb_s, vt_s = stream(x_sample, prev_s, past, tile_s)

    d_p = _pair_rows(_cumsum_log2(logf_p, tile=512))
    o_p = _flash(qb_p, kb_p, vt_p, d_p, d_p, tq=512, tk=512, q_off=0)
    y_p = _layer1(xp2, o_p, wo, row(g_ffn[1]), wg[1], wu[1], wd[1], tile=tile_p)

    sk = past + Ss
    sk_pad = -(-sk // MXU_DIM) * MXU_DIM
    n_pad = sk_pad - sk
    k_cache = jnp.transpose(cache_k.reshape(Bs, past, A // LANES, LANES), (0, 2, 1, 3)).astype(BF16)
    k_all = jnp.pad(jnp.concatenate([k_cache, kb_s], axis=2), ((0, 0), (0, 0), (0, n_pad), (0, 0)))
    vt_cache = jnp.transpose(cache_v.reshape(Bs, past, A), (0, 2, 1)).astype(BF16)
    vt_all = jnp.pad(jnp.concatenate([vt_cache, vt_s], axis=2), ((0, 0), (0, 0), (0, n_pad)))
    logf_all = jnp.pad(jnp.concatenate([cache_logf.astype(F32), logf_s], axis=1), ((0, 0), (0, n_pad), (0, 0)))
    d_all = _pair_rows(_cumsum_log2(logf_all, tile=sk_pad))
    o_s = _flash(qb_s, k_all, vt_all, d_all[:, :, :, past:sk], d_all, tq=Ss, tk=sk_pad, q_off=past)
    y_s = _layer1(xs2, o_s, wo, row(g_ffn[1]), wg[1], wu[1], wd[1], tile=tile_s)

    return (y_p, y_s, st_p[None], st_s[None],
            k_p.reshape(B, S, H, HEAD_DIM), v_p.reshape(B, S, H, HEAD_DIM), logf_p,
            k_s.reshape(Bs, Ss, H, HEAD_DIM), v_s.reshape(Bs, Ss, H, HEAD_DIM), logf_s)
```
